```python
import jax, jax.numpy as jnp
from jax import lax
import numpy as np

D_MODEL = 2048
BATCH = 2
SEQ = 4096
DEPTH = 1

CHUNK = 64
EPS = 1e-6
M_HEADS = 4
M_DV = 256
M_DQK = 128
M_WIDTH = M_HEADS * M_DV
M_QK_WIDTH = M_HEADS * M_DQK
C_WIDTH = 1024
C_GROUPS = 8
C_KERNEL = 3
D_FF = 5504
F_KERNEL = 3
SPLIT_SIZES = (M_QK_WIDTH, M_QK_WIDTH, M_WIDTH, M_WIDTH, M_HEADS, M_HEADS,
               C_WIDTH, C_WIDTH, C_WIDTH, D_MODEL, D_MODEL)
IN_WIDTH = sum(SPLIT_SIZES)
F_GATE_OFFSET = 2 * M_QK_WIDTH + 2 * M_WIDTH + M_HEADS

kernel_name = "hybrid_mlstm_shortconv_gated_block"


def rmsnorm(x, g):
    x32 = x.astype(jnp.float32)
    y = x32 * lax.rsqrt(jnp.mean(x32 * x32, axis=-1, keepdims=True) + EPS)
    return (y * g.astype(jnp.float32)).astype(x.dtype)


def causal_dwconv(x, w, b=None):
    K = w.shape[0]
    T = x.shape[1]
    xp = jnp.pad(x, ((0, 0), (K - 1, 0), (0, 0)))
    y = xp[:, 0:T, :] * w[0]
    for j in range(1, K):
        y = y + xp[:, j:j + T, :] * w[j]
    if b is not None:
        y = y + b
    return y


def mlstm_chunkwise(q, k, v, i_pre, f_pre):
    B, T, H, DQK = q.shape
    DV = v.shape[-1]
    NC = T // CHUNK
    L = CHUNK
    f32 = jnp.float32

    def to_chunks(a):
        a = jnp.moveaxis(a.astype(f32), 2, 1)
        return a.reshape((B, H, NC, L) + a.shape[3:])

    qc = to_chunks(q)
    kc = to_chunks(k) * (DQK ** -0.5)
    vc = to_chunks(v)
    logi = to_chunks(i_pre)
    logf = jax.nn.log_sigmoid(to_chunks(f_pre))

    b = jnp.cumsum(logf, axis=-1)
    a = b[..., -1]
    w = a[..., None] - b + logi
    m_loc = jnp.max(w, axis=-1)
    p = jnp.exp(w - m_loc[..., None])
    C_loc = jnp.einsum('bhcs,bhcsv,bhcsk->bhcvk', p, vc, kc)
    n_loc = jnp.einsum('bhcs,bhcsk->bhck', p, kc)

    def step(carry, xs):
        C, n, m = carry
        a_c, ml, Cl, nl = xs
        m_new = jnp.maximum(a_c + m, ml)
        s_old = jnp.exp(a_c + m - m_new)
        s_new = jnp.exp(ml - m_new)
        C_new = s_old[..., None, None] * C + s_new[..., None, None] * Cl
        n_new = s_old[..., None] * n + s_new[..., None] * nl
        return (C_new, n_new, m_new), (C, n, m)

    init = (jnp.zeros((B, H, DV, DQK), f32), jnp.zeros((B, H, DQK), f32), jnp.zeros((B, H), f32))
    xs = (jnp.moveaxis(a, 2, 0), jnp.moveaxis(m_loc, 2, 0),
          jnp.moveaxis(C_loc, 2, 0), jnp.moveaxis(n_loc, 2, 0))
    _, (C_prev, n_prev, m_prev) = lax.scan(step, init, xs)
    C_prev = jnp.moveaxis(C_prev, 0, 2)
    n_prev = jnp.moveaxis(n_prev, 0, 2)
    m_prev = jnp.moveaxis(m_prev, 0, 2)

    causal = jnp.tril(jnp.ones((L, L), dtype=bool))
    Dm = b[..., :, None] - b[..., None, :] + logi[..., None, :]
    Dm = jnp.where(causal, Dm, -jnp.inf)
    m_inter = b + m_prev[..., None]
    m_t = jnp.maximum(jnp.max(Dm, axis=-1), m_inter)
    S = jnp.einsum('bhctk,bhcsk->bhcts', qc, kc) * jnp.exp(Dm - m_t[..., None])
    inter_scale = jnp.exp(m_inter - m_t)
    num = (jnp.einsum('bhcts,bhcsv->bhctv', S, vc)
           + inter_scale[..., None] * jnp.einsum('bhcvk,bhctk->bhctv', C_prev, qc))
    den = jnp.sum(S, axis=-1) + inter_scale * jnp.einsum('bhck,bhctk->bhct', n_prev, qc)
    h = num / jnp.maximum(jnp.abs(den), jnp.exp(-m_t))[..., None]
    h = h.reshape(B, H, T, DV)
    return jnp.moveaxis(h, 1, 2)


def setup_inputs(seed: int = 0) -> dict:
    key = jax.random.key(seed)
    ks = jax.random.split(key, 16)
    f32 = jnp.float32
    nrm = lambda k, shape, scale: (jax.random.normal(k, shape, f32) * scale)
    x = jax.random.normal(ks[0], (BATCH, SEQ, D_MODEL), f32)
    norm_mix_g = 1.0 + nrm(ks[1], (D_MODEL,), 0.02)
    w_in = nrm(ks[2], (D_MODEL, IN_WIDTH), D_MODEL ** -0.5)
    b_in = nrm(ks[3], (IN_WIDTH,), 0.02)
    b_in = b_in.at[F_GATE_OFFSET:F_GATE_OFFSET + M_HEADS].add(jnp.linspace(3.0, 6.0, M_HEADS))
    mlstm_head_g = 1.0 + nrm(ks[4], (M_WIDTH,), 0.02)
    w_branch_m = nrm(ks[5], (M_WIDTH, D_MODEL), M_WIDTH ** -0.5)
    conv_w = nrm(ks[6], (C_KERNEL, C_WIDTH), C_KERNEL ** -0.5)
    w_branch_c = nrm(ks[7], (C_WIDTH, D_MODEL), C_WIDTH ** -0.5)
    w_out = nrm(ks[8], (D_MODEL, D_MODEL), D_MODEL ** -0.5)
    norm_ffn_g = 1.0 + nrm(ks[9], (D_MODEL,), 0.02)
    w_up = nrm(ks[10], (D_MODEL, 2 * D_FF), D_MODEL ** -0.5)
    ffn_conv_w = nrm(ks[11], (F_KERNEL, 2 * D_FF), F_KERNEL ** -0.5)
    ffn_conv_b = nrm(ks[12], (2 * D_FF,), 0.02)
    w_down = nrm(ks[13], (D_FF, D_MODEL), D_FF ** -0.5)
    norm_out_g = 1.0 + nrm(ks[14], (D_MODEL,), 0.02)
    return {"x": x, "norm_mix_g": norm_mix_g, "w_in": w_in, "b_in": b_in,
            "mlstm_head_g": mlstm_head_g, "w_branch_m": w_branch_m, "conv_w": conv_w,
            "w_branch_c": w_branch_c, "w_out": w_out, "norm_ffn_g": norm_ffn_g,
            "w_up": w_up, "ffn_conv_w": ffn_conv_w, "ffn_conv_b": ffn_conv_b,
            "w_down": w_down, "norm_out_g": norm_out_g}


def reference(x, norm_mix_g, w_in, b_in, mlstm_head_g, w_branch_m, conv_w, w_branch_c,
              w_out, norm_ffn_g, w_up, ffn_conv_w, ffn_conv_b, w_down, norm_out_g):
    B, T, _ = x.shape
    h = x
    for _layer in range(DEPTH):
        hn = rmsnorm(h, norm_mix_g)
        proj = hn @ w_in + b_in
        offs = [int(o) for o in np.cumsum(SPLIT_SIZES)[:-1]]
        (q, k, v, o_pre, i_pre, f_pre, cb, cc, ch, gm_pre, gc_pre) = jnp.split(proj, offs, axis=-1)

        hm = mlstm_chunkwise(q.reshape(B, T, M_HEADS, M_DQK), k.reshape(B, T, M_HEADS, M_DQK),
                             v.reshape(B, T, M_HEADS, M_DV), i_pre, f_pre)
        hm = hm * lax.rsqrt(jnp.mean(hm * hm, axis=-1, keepdims=True) + EPS)
        hm = hm.reshape(B, T, M_WIDTH) * mlstm_head_g.astype(jnp.float32)
        hm = hm.astype(x.dtype) * jax.nn.sigmoid(o_pre)
        y_m = hm @ w_branch_m

        y_c = (cb * causal_dwconv(cc * ch, conv_w)) @ w_branch_c

        merged = jax.nn.sigmoid(gm_pre) * y_m + jax.nn.sigmoid(gc_pre) * y_c
        h = h + merged @ w_out

        hn = rmsnorm(h, norm_ffn_g)
        u = causal_dwconv(hn @ w_up, ffn_conv_w, ffn_conv_b)
        gate, val = jnp.split(u, 2, axis=-1)
        h = h + (jax.nn.silu(gate) * val) @ w_down
    return rmsnorm(h, norm_out_g)
```

```python
import functools

import jax
import jax.numpy as jnp
from jax import lax
from jax.experimental import pallas as pl
from jax.experimental.pallas import tpu as pltpu

F32 = jnp.float32
BF16 = jnp.bfloat16

EPS = 1e-6
M_HEADS = 4
M_DV = 256
M_DQK = 128
M_WIDTH = M_HEADS * M_DV
M_QK_WIDTH = M_HEADS * M_DQK
C_WIDTH = 1024
QK_SCALE = M_DQK ** -0.5

LANES = 128
GATE_COLS = 2 * M_HEADS
MAIN_SPLIT = 2 * M_QK_WIDTH + 2 * M_WIDTH

VMEM_LIMIT = 56 * 1024 * 1024


def _params(n_axes, vmem=VMEM_LIMIT):
    return pltpu.CompilerParams(dimension_semantics=("arbitrary",) * n_axes,
                                vmem_limit_bytes=vmem)


def _sigmoid(v):
    return 1.0 / (1.0 + jnp.exp(-v))


def _inproj_kernel(x_ref, g_ref, w_ref, b_ref, wg_ref, bg_ref, proj_ref, gates_ref, hn_ref,
                   *, row_chunk):
    j = pl.program_id(1)
    tm = x_ref.shape[0]

    @pl.when(j == 0)
    def _():
        def body(r, carry):
            rows = pl.ds(pl.multiple_of(r * row_chunk, row_chunk), row_chunk)
            xb = x_ref[rows, :]
            ms = jnp.mean(xb * xb, axis=-1, keepdims=True)
            hn_ref[rows, :] = (xb * lax.rsqrt(ms + EPS) * g_ref[...]).astype(BF16)
            return carry
        lax.fori_loop(0, tm // row_chunk, body, 0)
        gates_ref[...] = (jnp.dot(hn_ref[...], wg_ref[...], preferred_element_type=F32)
                          + bg_ref[...])

    proj_ref[...] = (jnp.dot(hn_ref[...], w_ref[...], preferred_element_type=F32)
                     + b_ref[...]).astype(proj_ref.dtype)


def _inproj(x2, g, w_main, b_main, w_gate, b_gate, *, tm=1024, tn=1024):
    M, D = x2.shape
    N = w_main.shape[1]
    assert M % tm == 0 and N % tn == 0
    return pl.pallas_call(
        functools.partial(_inproj_kernel, row_chunk=128),
        grid=(M // tm, N // tn),
        in_specs=[
            pl.BlockSpec((tm, D), lambda i, j: (i, 0)),
            pl.BlockSpec((1, D), lambda i, j: (0, 0)),
            pl.BlockSpec((D, tn), lambda i, j: (0, j)),
            pl.BlockSpec((1, tn), lambda i, j: (0, j)),
            pl.BlockSpec((D, LANES), lambda i, j: (0, 0)),
            pl.BlockSpec((1, LANES), lambda i, j: (0, 0)),
        ],
        out_specs=[
            pl.BlockSpec((tm, tn), lambda i, j: (i, j)),
            pl.BlockSpec((tm, LANES), lambda i, j: (i, 0)),
        ],
        out_shape=[
            jax.ShapeDtypeStruct((M, N), BF16),
            jax.ShapeDtypeStruct((M, LANES), F32),
        ],
        scratch_shapes=[pltpu.VMEM((tm, D), BF16)],
        compiler_params=_params(2),
        name="inproj",
    )(x2, g, w_main, b_main, w_gate, b_gate)


def _split3(a):
    hi = a.astype(BF16)
    r1 = a - hi.astype(F32)
    mid = r1.astype(BF16)
    lo = (r1 - mid.astype(F32)).astype(BF16)
    return hi, mid, lo


def _mlstm_kernel(p_ref, g_ref, hg_ref, out_ref, st_ref, m_ref):
    c = pl.program_id(1)
    L = p_ref.shape[0]
    ext = M_DV + LANES

    @pl.when(c == 0)
    def _():
        st_ref[...] = jnp.zeros_like(st_ref)
        m_ref[...] = jnp.zeros_like(m_ref)

    X = g_ref[...]
    lane = lax.broadcasted_iota(jnp.int32, X.shape, 1)
    logsig = jnp.minimum(X, 0.0) - jnp.log1p(jnp.exp(-jnp.abs(X)))
    Xl = jnp.where((lane >= M_HEADS) & (lane < GATE_COLS), logsig, X)

    row = lax.broadcasted_iota(jnp.int32, (L, L), 0)
    col = lax.broadcasted_iota(jnp.int32, (L, L), 1)
    causal = row >= col
    tri = causal.astype(BF16)
    parts = _split3(Xl)
    Bc = (jnp.dot(tri, parts[2], preferred_element_type=F32)
          + jnp.dot(tri, parts[1], preferred_element_type=F32)
          + jnp.dot(tri, parts[0], preferred_element_type=F32))
    XT = Xl.T
    BT = Bc.T

    ones_blk = jnp.ones((L, LANES), BF16)
    for h in range(M_HEADS):
        i_col = Xl[:, h:h + 1]
        b_col = Bc[:, M_HEADS + h:M_HEADS + h + 1]
        i_row = XT[h:h + 1, :]
        b_row = BT[M_HEADS + h:M_HEADS + h + 1, :]
        m_prev = m_ref[h][0:1, 0:1]

        q = p_ref[:, h * M_DQK:(h + 1) * M_DQK]
        k = p_ref[:, M_QK_WIDTH + h * M_DQK:M_QK_WIDTH + (h + 1) * M_DQK]
        v = p_ref[:, 2 * M_QK_WIDTH + h * M_DV:2 * M_QK_WIDTH + (h + 1) * M_DV]
        o_pre = p_ref[:, 2 * M_QK_WIDTH + M_WIDTH + h * M_DV:
                      2 * M_QK_WIDTH + M_WIDTH + (h + 1) * M_DV].astype(F32)
        v_ext = jnp.concatenate([v, ones_blk], axis=1)

        Dm = jnp.where(causal, b_col - (b_row - i_row), -jnp.inf)
        m_inter = b_col + m_prev
        m_t = jnp.maximum(jnp.max(Dm, axis=-1, keepdims=True), m_inter)
        qk = lax.dot_general(q, k, (((1,), (1,)), ((), ())), preferred_element_type=F32)
        S = (qk * (QK_SCALE * jnp.exp(Dm - m_t))).astype(BF16)
        intra = jnp.dot(S, v_ext, preferred_element_type=F32)
        Ct = st_ref[h]
        inter = jnp.dot(q, Ct.astype(BF16), preferred_element_type=F32)
        tot = intra + (QK_SCALE * jnp.exp(m_inter - m_t)) * inter
        num = tot[:, :M_DV]
        den = tot[:, M_DV:M_DV + 1]
        hh = num / jnp.maximum(jnp.abs(den), jnp.exp(-m_t))

        hh = hh * lax.rsqrt(jnp.mean(hh * hh, axis=-1, keepdims=True) + EPS)
        hh = hh * hg_ref[:, h * M_DV:(h + 1) * M_DV]
        out_ref[:, h * M_DV:(h + 1) * M_DV] = (hh * _sigmoid(o_pre)).astype(out_ref.dtype)

        a = b_col[L - 1:L, :]
        w = a - b_col + i_col
        m_new = jnp.maximum(a + m_prev, jnp.max(w, axis=0, keepdims=True))
        kw = (k.astype(F32) * jnp.exp(w - m_new)).astype(BF16)
        upd = lax.dot_general(kw, v_ext, (((0,), (0,)), ((), ())), preferred_element_type=F32)
        st_ref[h] = jnp.exp(a + m_prev - m_new) * Ct + upd
        m_ref[h] = jnp.broadcast_to(m_new, m_ref.shape[1:])


def _mlstm(proj, gates, head_g, *, batch, seq, chunk=256):
    M = proj.shape[0]
    nc = seq // chunk
    assert seq % chunk == 0 and M == batch * seq
    width = 2 * M_QK_WIDTH + 2 * M_WIDTH
    return pl.pallas_call(
        _mlstm_kernel,
        grid=(batch, nc),
        in_specs=[
            pl.BlockSpec((chunk, width), lambda b, c: (b * nc + c, 0)),
            pl.BlockSpec((chunk, LANES), lambda b, c: (b * nc + c, 0)),
            pl.BlockSpec((1, M_WIDTH), lambda b, c: (0, 0)),
        ],
        out_specs=pl.BlockSpec((chunk, M_WIDTH), lambda b, c: (b * nc + c, 0)),
        out_shape=jax.ShapeDtypeStruct((M, M_WIDTH), BF16),
        scratch_shapes=[
            pltpu.VMEM((M_HEADS, M_DQK, M_DV + LANES), F32),
            pltpu.VMEM((M_HEADS, 8, LANES), F32),
        ],
        compiler_params=_params(2),
        name="mlstm",
    )(proj, gates, head_g)


def _causal_conv3(cur, prev2, prev1, w_ref):
    row = lax.broadcasted_iota(jnp.int32, cur.shape, 0)
    r1 = jnp.where(row == 0, prev1, pltpu.roll(cur, 1, 0))
    r2 = jnp.where(row == 0, prev2, jnp.where(row == 1, prev1, pltpu.roll(cur, 2, 0)))
    return r2 * w_ref[0:1, :] + r1 * w_ref[1:2, :] + cur * w_ref[2:3, :]


def _mix_kernel(hm_ref, cb_ref, halo_ref, gm_ref, gc_ref, x_ref, cw_ref, wbm_ref, wbc_ref,
                wout_ref, g2_ref, h1_ref, hn2_ref, merged_ref, *, blocks_per_seq, col_chunk):
    i = pl.program_id(0)
    tm = hm_ref.shape[0]
    D = x_ref.shape[1]

    cb = cb_ref[:, 0:C_WIDTH].astype(F32)
    pp = cb_ref[:, C_WIDTH:2 * C_WIDTH].astype(F32) * cb_ref[:, 2 * C_WIDTH:3 * C_WIDTH].astype(F32)
    hp = halo_ref[:, 0:C_WIDTH].astype(F32) * halo_ref[:, C_WIDTH:2 * C_WIDTH].astype(F32)
    hp = jnp.where(i % blocks_per_seq != 0, hp, 0.0)
    nh = halo_ref.shape[0]
    z = (cb * _causal_conv3(pp, hp[nh - 2:nh - 1, :], hp[nh - 1:nh, :], cw_ref)).astype(BF16)
    hm = hm_ref[...]

    for n in range(D // col_chunk):
        cols = slice(n * col_chunk, (n + 1) * col_chunk)
        y_m = jnp.dot(hm, wbm_ref[:, cols], preferred_element_type=F32)
        y_c = jnp.dot(z, wbc_ref[:, cols], preferred_element_type=F32)
        merged_ref[:, cols] = (_sigmoid(gm_ref[:, cols].astype(F32)) * y_m
                               + _sigmoid(gc_ref[:, cols].astype(F32)) * y_c).astype(BF16)

    ssq = jnp.zeros((tm, 1), F32)
    for n in range(D // col_chunk):
        cols = slice(n * col_chunk, (n + 1) * col_chunk)
        h1 = x_ref[:, cols] + jnp.dot(merged_ref[...], wout_ref[:, cols],
                                      preferred_element_type=F32)
        h1_ref[:, cols] = h1
        ssq = ssq + jnp.sum(h1 * h1, axis=-1, keepdims=True)
    rs = lax.rsqrt(ssq / D + EPS)
    for n in range(D // col_chunk):
        cols = slice(n * col_chunk, (n + 1) * col_chunk)
        hn2_ref[:, cols] = (h1_ref[:, cols] * rs * g2_ref[:, cols]).astype(hn2_ref.dtype)


def _mix(hm, proj, x2, conv_w, w_bm, w_bc, w_out, g2, *, seq, tm=256, halo=16):
    M, D = x2.shape
    assert M % tm == 0 and seq % tm == 0 and tm % halo == 0
    cb_blk = MAIN_SPLIT // (3 * C_WIDTH)
    assert cb_blk * 3 * C_WIDTH == MAIN_SPLIT
    cc_off = MAIN_SPLIT + C_WIDTH
    gm_off = MAIN_SPLIT + 3 * C_WIDTH
    assert cc_off % (2 * C_WIDTH) == 0 and gm_off % D == 0
    const = dict(pipeline_mode=pl.Buffered(1))
    return pl.pallas_call(
        functools.partial(_mix_kernel, blocks_per_seq=seq // tm, col_chunk=512),
        grid=(M // tm,),
        in_specs=[
            pl.BlockSpec((tm, M_WIDTH), lambda i: (i, 0)),
            pl.BlockSpec((tm, 3 * C_WIDTH), lambda i: (i, cb_blk)),
            pl.BlockSpec((halo, 2 * C_WIDTH),
                         lambda i: (jnp.maximum(i * (tm // halo) - 1, 0), cc_off // (2 * C_WIDTH))),
            pl.BlockSpec((tm, D), lambda i: (i, gm_off // D)),
            pl.BlockSpec((tm, D), lambda i: (i, gm_off // D + 1)),
            pl.BlockSpec((tm, D), lambda i: (i, 0)),
            pl.BlockSpec(conv_w.shape, lambda i: (0, 0)),
            pl.BlockSpec(w_bm.shape, lambda i: (0, 0), **const),
            pl.BlockSpec(w_bc.shape, lambda i: (0, 0), **const),
            pl.BlockSpec(w_out.shape, lambda i: (0, 0), **const),
            pl.BlockSpec((1, D), lambda i: (0, 0)),
        ],
        out_specs=[
            pl.BlockSpec((tm, D), lambda i: (i, 0)),
            pl.BlockSpec((tm, D), lambda i: (i, 0)),
        ],
        out_shape=[
            jax.ShapeDtypeStruct((M, D), F32),
            jax.ShapeDtypeStruct((M, D), BF16),
        ],
        scratch_shapes=[pltpu.VMEM((tm, D), BF16)],
        compiler_params=_params(1),
        name="mix",
    )(hm, proj, proj, proj, proj, x2, conv_w, w_bm, w_bc, w_out, g2)


def _ffn_kernel(hn_ref, h1_ref, wg_ref, wv_ref, cwg_ref, cwv_ref, cbg_ref, cbv_ref, wd_ref,
                g3_ref, out_ref, carry_ref, *, blocks_per_seq):
    i = pl.program_id(0)
    j = pl.program_id(1)
    nj = pl.num_programs(1)
    tm, tn = hn_ref.shape[0], wg_ref.shape[1]
    D = out_ref.shape[1]

    a = hn_ref[...]
    @pl.when(i % blocks_per_seq == 0)
    def _():
        carry_ref[j] = jnp.zeros(carry_ref.shape[1:], F32)

    prev = carry_ref[j]

    def branch(w_ref, cw_ref, cb_ref, lo):
        u = jnp.dot(a, w_ref[...], preferred_element_type=F32)
        carry_ref[j, :, lo:lo + tn] = u[tm - 8:tm, :]
        return _causal_conv3(u, prev[6:7, lo:lo + tn], prev[7:8, lo:lo + tn], cw_ref) + cb_ref[...]

    gate = branch(wg_ref, cwg_ref, cbg_ref, 0)
    val = branch(wv_ref, cwv_ref, cbv_ref, tn)
    act = (gate * _sigmoid(gate) * val).astype(BF16)
    down = jnp.dot(act, wd_ref[...], preferred_element_type=F32)

    @pl.when(j == 0)
    def _():
        out_ref[...] = h1_ref[...] + down

    @pl.when(j > 0)
    def _():
        out_ref[...] += down

    @pl.when(j == nj - 1)
    def _():
        h2 = out_ref[...]
        rs = lax.rsqrt(jnp.mean(h2 * h2, axis=-1, keepdims=True) + EPS)
        out_ref[...] = h2 * rs * g3_ref[...]


def _ffn(hn2, h1, w_up_p, cw_p, cb_p, w_down_p, g3, *, seq, tm=512, tn=512):
    M, D = h1.shape
    ffp = w_down_p.shape[0]
    assert M % tm == 0 and seq % tm == 0 and ffp % tn == 0
    nj = ffp // tn
    return pl.pallas_call(
        functools.partial(_ffn_kernel, blocks_per_seq=seq // tm),
        grid=(M // tm, nj),
        in_specs=[
            pl.BlockSpec((tm, D), lambda i, j: (i, 0)),
            pl.BlockSpec((tm, D), lambda i, j: (i, 0)),
            pl.BlockSpec((D, tn), lambda i, j: (0, j)),
            pl.BlockSpec((D, tn), lambda i, j: (0, nj + j)),
            pl.BlockSpec((3, tn), lambda i, j: (0, j)),
            pl.BlockSpec((3, tn), lambda i, j: (0, nj + j)),
            pl.BlockSpec((1, tn), lambda i, j: (0, j)),
            pl.BlockSpec((1, tn), lambda i, j: (0, nj + j)),
            pl.BlockSpec((tn, D), lambda i, j: (j, 0)),
            pl.BlockSpec((1, D), lambda i, j: (0, 0)),
        ],
        out_specs=pl.BlockSpec((tm, D), lambda i, j: (i, 0)),
        out_shape=jax.ShapeDtypeStruct((M, D), F32),
        scratch_shapes=[pltpu.VMEM((nj, 8, 2 * tn), F32)],
        compiler_params=_params(2),
        name="ffn",
    )(hn2, h1, w_up_p, w_up_p, cw_p, cw_p, cb_p, cb_p, w_down_p, g3)


def _pad_cols(a, width):
    return jnp.pad(a, ((0, 0), (0, width - a.shape[1])))


def kernel(x, norm_mix_g, w_in, b_in, mlstm_head_g, w_branch_m, conv_w, w_branch_c, w_out,
           norm_ffn_g, w_up, ffn_conv_w, ffn_conv_b, w_down, norm_out_g):
    B, T, D = x.shape
    M = B * T
    d_ff = w_down.shape[0]
    ff_tile = 512
    ffp = pl.cdiv(d_ff, ff_tile) * ff_tile
    x2 = x.reshape(M, D)

    g0, g1 = MAIN_SPLIT, MAIN_SPLIT + GATE_COLS
    w_main = jnp.concatenate([w_in[:, :g0], w_in[:, g1:]], axis=1).astype(BF16)
    b_main = jnp.concatenate([b_in[:g0], b_in[g1:]])[None, :]
    w_gate = _pad_cols(w_in[:, g0:g1], LANES).astype(BF16)
    b_gate = _pad_cols(b_in[None, g0:g1], LANES)
    halves = lambda a: jnp.concatenate([_pad_cols(a[:, :d_ff], ffp), _pad_cols(a[:, d_ff:], ffp)], axis=1)
    w_up_p = halves(w_up).astype(BF16)
    cw_p = halves(ffn_conv_w)
    cb_p = halves(ffn_conv_b[None, :])
    w_down_p = jnp.pad(w_down, ((0, ffp - d_ff), (0, 0))).astype(BF16)

    proj, gates = _inproj(x2, norm_mix_g[None, :], w_main, b_main, w_gate, b_gate)
    hm = _mlstm(proj, gates, mlstm_head_g[None, :], batch=B, seq=T)
    h1, hn2 = _mix(hm, proj, x2, conv_w, w_branch_m.astype(BF16), w_branch_c.astype(BF16),
                   w_out.astype(BF16), norm_ffn_g[None, :], seq=T)
    out = _ffn(hn2, h1, w_up_p, cw_p, cb_p, w_down_p, norm_out_g[None, :], seq=T, tn=ff_tile)
    return out.reshape(B, T, D)
```

```python
import functools

import jax
import jax.numpy as jnp
from jax import lax
from jax.experimental import pallas as pl
from jax.experimental.pallas import tpu as pltpu

F32 = jnp.float32
BF16 = jnp.bfloat16

EPS = 1e-6
M_HEADS = 4
M_DV = 256
M_DQK = 128
M_WIDTH = M_HEADS * M_DV
M_QK_WIDTH = M_HEADS * M_DQK
C_WIDTH = 1024
QK_SCALE = M_DQK ** -0.5

LANES = 128
GATE_COLS = 2 * M_HEADS
MAIN_SPLIT = 2 * M_QK_WIDTH + 2 * M_WIDTH

VMEM_LIMIT = 56 * 1024 * 1024


def _params(n_axes, vmem=VMEM_LIMIT):
    return pltpu.CompilerParams(dimension_semantics=("arbitrary",) * n_axes,
                                vmem_limit_bytes=vmem)


def _sigmoid(v):
    return 1.0 / (1.0 + jnp.exp(-v))


def _inproj_kernel(x_ref, g_ref, w_ref, b_ref, wg_ref, bg_ref, proj_ref, gates_ref, hn_ref,
                   *, row_chunk):
    j = pl.program_id(1)
    tm = x_ref.shape[0]

    @pl.when(j == 0)
    def _():
        def body(r, carry):
            rows = pl.ds(pl.multiple_of(r * row_chunk, row_chunk), row_chunk)
            xb = x_ref[rows, :]
            ms = jnp.mean(xb * xb, axis=-1, keepdims=True)
            hn_ref[rows, :] = (xb * lax.rsqrt(ms + EPS) * g_ref[...]).astype(BF16)
            return carry
        lax.fori_loop(0, tm // row_chunk, body, 0)
        gates_ref[...] = (jnp.dot(hn_ref[...], wg_ref[...], preferred_element_type=F32)
                          + bg_ref[...])

    proj_ref[...] = (jnp.dot(hn_ref[...], w_ref[...], preferred_element_type=F32)
                     + b_ref[...]).astype(proj_ref.dtype)


def _inproj(x2, g, w_main, b_main, w_gate, b_gate, *, tm=1024, tn=1024):
    M, D = x2.shape
    N = w_main.shape[1]
    assert M % tm == 0 and N % tn == 0
    return pl.pallas_call(
        functools.partial(_inproj_kernel, row_chunk=128),
        grid=(M // tm, N // tn),
        in_specs=[
            pl.BlockSpec((tm, D), lambda i, j: (i, 0)),
            pl.BlockSpec((1, D), lambda i, j: (0, 0)),
            pl.BlockSpec((D, tn), lambda i, j: (0, j)),
            pl.BlockSpec((1, tn), lambda i, j: (0, j)),
            pl.BlockSpec((D, LANES), lambda i, j: (0, 0)),
            pl.BlockSpec((1, LANES), lambda i, j: (0, 0)),
        ],
        out_specs=[
            pl.BlockSpec((tm, tn), lambda i, j: (i, j)),
            pl.BlockSpec((tm, LANES), lambda i, j: (i, 0)),
        ],
        out_shape=[
            jax.ShapeDtypeStruct((M, N), BF16),
            jax.ShapeDtypeStruct((M, LANES), F32),
        ],
        scratch_shapes=[pltpu.VMEM((tm, D), BF16)],
        compiler_params=_params(2),
        name="inproj",
    )(x2, g, w_main, b_main, w_gate, b_gate)


def _split3(a):
    hi = a.astype(BF16)
    r1 = a - hi.astype(F32)
    mid = r1.astype(BF16)
    lo = (r1 - mid.astype(F32)).astype(BF16)
    return hi, mid, lo


def _mlstm_kernel(p_ref, g_ref, hg_ref, out_ref, st_ref, m_ref):
    c = pl.program_id(1)
    L = p_ref.shape[0]
    ext = M_DV + LANES

    @pl.when(c == 0)
    def _():
        st_ref[...] = jnp.zeros_like(st_ref)
        m_ref[...] = jnp.zeros_like(m_ref)

    X = g_ref[...]
    lane = lax.broadcasted_iota(jnp.int32, X.shape, 1)
    logsig = jnp.minimum(X, 0.0) - jnp.log1p(jnp.exp(-jnp.abs(X)))
    Xl = jnp.where((lane >= M_HEADS) & (lane < GATE_COLS), logsig, X)

    row = lax.broadcasted_iota(jnp.int32, (L, L), 0)
    col = lax.broadcasted_iota(jnp.int32, (L, L), 1)
    causal = row >= col
    tri = causal.astype(BF16)
    parts = _split3(Xl)
    Bc = (jnp.dot(tri, parts[2], preferred_element_type=F32)
          + jnp.dot(tri, parts[1], preferred_element_type=F32)
          + jnp.dot(tri, parts[0], preferred_element_type=F32))
    XT = Xl.T
    BT = Bc.T

    ones_blk = jnp.ones((L, LANES), BF16)
    for h in range(M_HEADS):
        i_col = Xl[:, h:h + 1]
        b_col = Bc[:, M_HEADS + h:M_HEADS + h + 1]
        i_row = XT[h:h + 1, :]
        b_row = BT[M_HEADS + h:M_HEADS + h + 1, :]
        m_prev = m_ref[h][0:1, 0:1]

        q = p_ref[:, h * M_DQK:(h + 1) * M_DQK]
        k = p_ref[:, M_QK_WIDTH + h * M_DQK:M_QK_WIDTH + (h + 1) * M_DQK]
        v = p_ref[:, 2 * M_QK_WIDTH + h * M_DV:2 * M_QK_WIDTH + (h + 1) * M_DV]
        o_pre = p_ref[:, 2 * M_QK_WIDTH + M_WIDTH + h * M_DV:
                      2 * M_QK_WIDTH + M_WIDTH + (h + 1) * M_DV].astype(F32)
        v_ext = jnp.concatenate([v, ones_blk], axis=1)

        Dm = jnp.where(causal, b_col - (b_row - i_row), -jnp.inf)
        m_inter = b_col + m_prev
        m_t = jnp.maximum(jnp.max(Dm, axis=-1, keepdims=True), m_inter)
        qk = lax.dot_general(q, k, (((1,), (1,)), ((), ())), preferred_element_type=F32)
        S = (qk * (QK_SCALE * jnp.exp(Dm - m_t))).astype(BF16)
        intra = jnp.dot(S, v_ext, preferred_element_type=F32)
        Ct = st_ref[h]
        inter = jnp.dot(q, Ct.astype(BF16), preferred_element_type=F32)
        tot = intra + (QK_SCALE * jnp.exp(m_inter - m_t)) * inter
        num = tot[:, :M_DV]
        den = tot[:, M_DV:M_DV + 1]
        hh = num / jnp.maximum(jnp.abs(den), jnp.exp(-m_t))

        hh = hh * lax.rsqrt(jnp.mean(hh * hh, axis=-1, keepdims=True) + EPS)
        hh = hh * hg_ref[:, h * M_DV:(h + 1) * M_DV]
        out_ref[:, h * M_DV:(h + 1) * M_DV] = (hh * _sigmoid(o_pre)).astype(out_ref.dtype)

        a = b_col[L - 1:L, :]
        w = a - b_col + i_col
        m_new = jnp.maximum(a + m_prev, jnp.max(w, axis=0, keepdims=True))
        kw = (k.astype(F32) * jnp.exp(w - m_new)).astype(BF16)
        upd = lax.dot_general(kw, v_ext, (((0,), (0,)), ((), ())), preferred_element_type=F32)
        st_ref[h] = jnp.exp(a + m_prev - m_new) * Ct + upd
        m_ref[h] = jnp.broadcast_to(m_new, m_ref.shape[1:])


def _mlstm(proj, gates, head_g, *, batch, seq, chunk=256):
    M = proj.shape[0]
    nc = seq // chunk
    assert seq % chunk == 0 and M == batch * seq
    width = 2 * M_QK_WIDTH + 2 * M_WIDTH
    return pl.pallas_call(
        _mlstm_kernel,
        grid=(batch, nc),
        in_specs=[
            pl.BlockSpec((chunk, width), lambda b, c: (b * nc + c, 0)),
            pl.BlockSpec((chunk, LANES), lambda b, c: (b * nc + c, 0)),
            pl.BlockSpec((1, M_WIDTH), lambda b, c: (0, 0)),
        ],
        out_specs=pl.BlockSpec((chunk, M_WIDTH), lambda b, c: (b * nc + c, 0)),
        out_shape=jax.ShapeDtypeStruct((M, M_WIDTH), BF16),
        scratch_shapes=[
            pltpu.VMEM((M_HEADS, M_DQK, M_DV + LANES), F32),
            pltpu.VMEM((M_HEADS, 8, LANES), F32),
        ],
        compiler_params=_params(2),
        name="mlstm",
    )(proj, gates, head_g)


def _causal_conv3(cur, prev2, prev1, w_ref):
    row = lax.broadcasted_iota(jnp.int32, cur.shape, 0)
    r1 = jnp.where(row == 0, prev1, pltpu.roll(cur, 1, 0))
    r2 = jnp.where(row == 0, prev2, jnp.where(row == 1, prev1, pltpu.roll(cur, 2, 0)))
    return r2 * w_ref[0:1, :] + r1 * w_ref[1:2, :] + cur * w_ref[2:3, :]


def _mix_kernel(hm_ref, cb_ref, halo_ref, gm_ref, gc_ref, x_ref, cw_ref, wbm_ref, wbc_ref,
                wout_ref, g2_ref, h1_ref, hn2_ref, merged_ref, *, blocks_per_seq, col_chunk):
    i = pl.program_id(0)
    tm = hm_ref.shape[0]
    D = x_ref.shape[1]

    cb = cb_ref[:, 0:C_WIDTH].astype(F32)
    pp = cb_ref[:, C_WIDTH:2 * C_WIDTH].astype(F32) * cb_ref[:, 2 * C_WIDTH:3 * C_WIDTH].astype(F32)
    hp = halo_ref[:, 0:C_WIDTH].astype(F32) * halo_ref[:, C_WIDTH:2 * C_WIDTH].astype(F32)
    hp = jnp.where(i % blocks_per_seq != 0, hp, 0.0)
    nh = halo_ref.shape[0]
    z = (cb * _causal_conv3(pp, hp[nh - 2:nh - 1, :], hp[nh - 1:nh, :], cw_ref)).astype(BF16)
    hm = hm_ref[...]

    for n in range(D // col_chunk):
        cols = slice(n * col_chunk, (n + 1) * col_chunk)
        y_m = jnp.dot(hm, wbm_ref[:, cols], preferred_element_type=F32)
        y_c = jnp.dot(z, wbc_ref[:, cols], preferred_element_type=F32)
        merged_ref[:, cols] = (_sigmoid(gm_ref[:, cols].astype(F32)) * y_m
                               + _sigmoid(gc_ref[:, cols].astype(F32)) * y_c).astype(BF16)

    ssq = jnp.zeros((tm, 1), F32)
    for n in range(D // col_chunk):
        cols = slice(n * col_chunk, (n + 1) * col_chunk)
        h1 = x_ref[:, cols] + jnp.dot(merged_ref[...], wout_ref[:, cols],
                                      preferred_element_type=F32)
        h1_ref[:, cols] = h1
        ssq = ssq + jnp.sum(h1 * h1, axis=-1, keepdims=True)
    rs = lax.rsqrt(ssq / D + EPS)
    for n in range(D // col_chunk):
        cols = slice(n * col_chunk, (n + 1) * col_chunk)
        hn2_ref[:, cols] = (h1_ref[:, cols] * rs * g2_ref[:, cols]).astype(hn2_ref.dtype)


def _mix(hm, proj, x2, conv_w, w_bm, w_bc, w_out, g2, *, seq, tm=256, halo=16):
    M, D = x2.shape
    assert M % tm == 0 and seq % tm == 0 and tm % halo == 0
    cb_blk = MAIN_SPLIT // (3 * C_WIDTH)
    assert cb_blk * 3 * C_WIDTH == MAIN_SPLIT
    cc_off = MAIN_SPLIT + C_WIDTH
    gm_off = MAIN_SPLIT + 3 * C_WIDTH
    assert cc_off % (2 * C_WIDTH) == 0 and gm_off % D == 0
    const = dict(pipeline_mode=pl.Buffered(1))
    return pl.pallas_call(
        functools.partial(_mix_kernel, blocks_per_seq=seq // tm, col_chunk=512),
        grid=(M // tm,),
        in_specs=[
            pl.BlockSpec((tm, M_WIDTH), lambda i: (i, 0)),
            pl.BlockSpec((tm, 3 * C_WIDTH), lambda i: (i, cb_blk)),
            pl.BlockSpec((halo, 2 * C_WIDTH),
                         lambda i: (jnp.maximum(i * (tm // halo) - 1, 0), cc_off // (2 * C_WIDTH))),
            pl.BlockSpec((tm, D), lambda i: (i, gm_off // D)),
            pl.BlockSpec((tm, D), lambda i: (i, gm_off // D + 1)),
            pl.BlockSpec((tm, D), lambda i: (i, 0)),
            pl.BlockSpec(conv_w.shape, lambda i: (0, 0)),
            pl.BlockSpec(w_bm.shape, lambda i: (0, 0), **const),
            pl.BlockSpec(w_bc.shape, lambda i: (0, 0), **const),
            pl.BlockSpec(w_out.shape, lambda i: (0, 0), **const),
            pl.BlockSpec((1, D), lambda i: (0, 0)),
        ],
        out_specs=[
            pl.BlockSpec((tm, D), lambda i: (i, 0)),
            pl.BlockSpec((tm, D), lambda i: (i, 0)),
        ],
        out_shape=[
            jax.ShapeDtypeStruct((M, D), F32),
            jax.ShapeDtypeStruct((M, D), BF16),
        ],
        scratch_shapes=[pltpu.VMEM((tm, D), BF16)],
        compiler_params=_params(1),
        name="mix",
    )(hm, proj, proj, proj, proj, x2, conv_w, w_bm, w_bc, w_out, g2)


def _ffn_kernel(hn_ref, h1_ref, wu_ref, cw_ref, cb_ref, wd_ref, g3_ref, out_ref,
                u_ref, a_ref, carry_ref, *, nj, blocks_per_seq):
    s = pl.program_id(0)
    tm, tn = hn_ref.shape[0], wd_ref.shape[0]
    tile_b = jnp.maximum(s - 1, 0)
    ib, jb = tile_b // nj, tile_b % nj
    jc = (s + nj - 2) % nj

    @pl.when(s == 0)
    def _():
        u_ref[...] = jnp.zeros_like(u_ref)
        a_ref[...] = jnp.zeros_like(a_ref)
        carry_ref[...] = jnp.zeros_like(carry_ref)

    @pl.when((jc == 0) | (s == 0))
    def _():
        out_ref[...] = jnp.zeros_like(out_ref)

    out_ref[...] += jnp.dot(a_ref[...], wd_ref[...], preferred_element_type=F32)

    u = u_ref[...]
    prev = jnp.where(ib % blocks_per_seq == 0, 0.0, carry_ref[jb])
    carry_ref[jb] = u[tm - 8:tm, :]
    y = _causal_conv3(u, prev[6:7, :], prev[7:8, :], cw_ref) + cb_ref[...]
    gate, val = y[:, :tn], y[:, tn:]
    a_ref[...] = (gate * _sigmoid(gate) * val).astype(BF16)

    u_ref[...] = jnp.dot(hn_ref[...], wu_ref[...], preferred_element_type=F32)

    @pl.when((jc == nj - 1) & (s >= 2))
    def _():
        h2 = out_ref[...] + h1_ref[...]
        rs = lax.rsqrt(jnp.mean(h2 * h2, axis=-1, keepdims=True) + EPS)
        out_ref[...] = h2 * rs * g3_ref[...]


def _ffn(hn2, h1, w_up_t, cw_t, cb_t, w_down_p, g3, *, seq, tm=512):
    M, D = h1.shape
    nj, _, tn2 = w_up_t.shape
    tn = tn2 // 2
    assert M % tm == 0 and seq % tm == 0 and w_down_p.shape[0] == nj * tn
    ni = M // tm
    n_tiles = ni * nj
    row_a = lambda s: (jnp.minimum(s // nj, ni - 1), 0)
    row_c = lambda s: (jnp.maximum(s - 2, 0) // nj, 0)
    tile_a = lambda s: (s % nj, 0, 0)
    tile_b = lambda s: (jnp.maximum(s - 1, 0) % nj, 0, 0)
    return pl.pallas_call(
        functools.partial(_ffn_kernel, nj=nj, blocks_per_seq=seq // tm),
        grid=(n_tiles + 2,),
        in_specs=[
            pl.BlockSpec((tm, D), row_a),
            pl.BlockSpec((tm, D), row_c),
            pl.BlockSpec((None, D, tn2), tile_a),
            pl.BlockSpec((None, 3, tn2), tile_b),
            pl.BlockSpec((None, 1, tn2), tile_b),
            pl.BlockSpec((tn, D), lambda s: ((s + nj - 2) % nj, 0)),
            pl.BlockSpec((1, D), lambda s: (0, 0)),
        ],
        out_specs=pl.BlockSpec((tm, D), row_c),
        out_shape=jax.ShapeDtypeStruct((M, D), F32),
        scratch_shapes=[
            pltpu.VMEM((tm, tn2), F32),
            pltpu.VMEM((tm, tn), BF16),
            pltpu.VMEM((nj, 8, tn2), F32),
        ],
        compiler_params=_params(1),
        name="ffn",
    )(hn2, h1, w_up_t, cw_t, cb_t, w_down_p, g3)


def _win_prep_kernel(w_ref, main_ref, gate_ref):
    g0, g1 = MAIN_SPLIT, MAIN_SPLIT + GATE_COLS
    n = w_ref.shape[1]
    main_ref[:, :g0] = w_ref[:, :g0].astype(BF16)
    main_ref[:, g0:] = w_ref[:, g1:n].astype(BF16)
    g = w_ref[:, g0:g0 + LANES]
    lane = lax.broadcasted_iota(jnp.int32, g.shape, 1)
    gate_ref[...] = jnp.where(lane < GATE_COLS, g, 0.0).astype(BF16)


def _win_prep(w_in, *, rows=128):
    D, n = w_in.shape
    assert D % rows == 0
    return pl.pallas_call(
        _win_prep_kernel,
        grid=(D // rows,),
        in_specs=[pl.BlockSpec((rows, n), lambda r: (r, 0))],
        out_specs=[
            pl.BlockSpec((rows, n - GATE_COLS), lambda r: (r, 0)),
            pl.BlockSpec((rows, LANES), lambda r: (r, 0)),
        ],
        out_shape=[
            jax.ShapeDtypeStruct((D, n - GATE_COLS), BF16),
            jax.ShapeDtypeStruct((D, LANES), BF16),
        ],
        compiler_params=_params(1),
        name="win_prep",
    )(w_in)


def _wup_prep_kernel(w_ref, o_ref, *, d_ff, tn):
    nj = o_ref.shape[0]
    for j in range(nj):
        lo = j * tn
        width = min(tn, d_ff - lo)
        o_ref[j, :, 0:width] = w_ref[:, lo:lo + width].astype(BF16)
        o_ref[j, :, tn:tn + width] = w_ref[:, d_ff + lo:d_ff + lo + width].astype(BF16)
        if width < tn:
            pad = jnp.zeros((o_ref.shape[1], tn - width), BF16)
            o_ref[j, :, width:tn] = pad
            o_ref[j, :, tn + width:2 * tn] = pad


def _wup_prep(w_up, *, d_ff, tn, rows=128):
    D = w_up.shape[0]
    nj = pl.cdiv(d_ff, tn)
    assert D % rows == 0 and d_ff % LANES == 0 and tn % LANES == 0
    return pl.pallas_call(
        functools.partial(_wup_prep_kernel, d_ff=d_ff, tn=tn),
        grid=(D // rows,),
        in_specs=[pl.BlockSpec((rows, 2 * d_ff), lambda r: (r, 0))],
        out_specs=pl.BlockSpec((nj, rows, 2 * tn), lambda r: (0, r, 0)),
        out_shape=jax.ShapeDtypeStruct((nj, D, 2 * tn), BF16),
        compiler_params=_params(1),
        name="wup_prep",
    )(w_up)


def _tile_pairs(a, d_ff, tn):
    nj = pl.cdiv(d_ff, tn)
    pad = lambda h: jnp.pad(h, ((0, 0), (0, nj * tn - d_ff))).reshape(a.shape[0], nj, tn)
    both = jnp.concatenate([pad(a[:, :d_ff]), pad(a[:, d_ff:])], axis=2)
    return jnp.transpose(both, (1, 0, 2))


def kernel(x, norm_mix_g, w_in, b_in, mlstm_head_g, w_branch_m, conv_w, w_branch_c, w_out,
           norm_ffn_g, w_up, ffn_conv_w, ffn_conv_b, w_down, norm_out_g):
    B, T, D = x.shape
    M = B * T
    d_ff = w_down.shape[0]
    ff_tile = 512
    ffp = pl.cdiv(d_ff, ff_tile) * ff_tile
    x2 = x.reshape(M, D)

    g0, g1 = MAIN_SPLIT, MAIN_SPLIT + GATE_COLS
    w_main, w_gate = _win_prep(w_in)
    b_main = jnp.concatenate([b_in[:g0], b_in[g1:]])[None, :]
    b_gate = jnp.pad(b_in[None, g0:g1], ((0, 0), (0, LANES - GATE_COLS)))
    w_up_t = _wup_prep(w_up, d_ff=d_ff, tn=ff_tile)
    cw_t = _tile_pairs(ffn_conv_w, d_ff, ff_tile)
    cb_t = _tile_pairs(ffn_conv_b[None, :], d_ff, ff_tile)
    w_down_p = jnp.pad(w_down.astype(BF16), ((0, ffp - d_ff), (0, 0)))

    proj, gates = _inproj(x2, norm_mix_g[None, :], w_main, b_main, w_gate, b_gate)
    hm = _mlstm(proj, gates, mlstm_head_g[None, :], batch=B, seq=T)
    h1, hn2 = _mix(hm, proj, x2, conv_w, w_branch_m.astype(BF16), w_branch_c.astype(BF16),
                   w_out.astype(BF16), norm_ffn_g[None, :], seq=T)
    out = _ffn(hn2, h1, w_up_t, cw_t, cb_t, w_down_p, norm_out_g[None, :], seq=T)
    return out.reshape(B, T, D)
```

```python
import functools

import jax
import jax.numpy as jnp
from jax import lax
from jax.experimental import pallas as pl
from jax.experimental.pallas import tpu as pltpu

F32 = jnp.float32
BF16 = jnp.bfloat16

EPS = 1e-6
M_HEADS = 4
M_DV = 256
M_DQK = 128
M_WIDTH = M_HEADS * M_DV
M_QK_WIDTH = M_HEADS * M_DQK
C_WIDTH = 1024
QK_SCALE = M_DQK ** -0.5

LANES = 128
GATE_COLS = 2 * M_HEADS
MAIN_SPLIT = 2 * M_QK_WIDTH + 2 * M_WIDTH

VMEM_LIMIT = 56 * 1024 * 1024


def _params(n_axes, vmem=VMEM_LIMIT):
    return pltpu.CompilerParams(dimension_semantics=("arbitrary",) * n_axes,
                                vmem_limit_bytes=vmem)


def _sigmoid(v):
    return 1.0 / (1.0 + jnp.exp(-v))


def _inproj_kernel(x_ref, g_ref, w_ref, b_ref, wg_ref, bg_ref, proj_ref, gates_ref, hn_ref,
                   *, row_chunk):
    j = pl.program_id(1)
    tm = x_ref.shape[0]

    @pl.when(j == 0)
    def _():
        def body(r, carry):
            rows = pl.ds(pl.multiple_of(r * row_chunk, row_chunk), row_chunk)
            xb = x_ref[rows, :]
            ms = jnp.mean(xb * xb, axis=-1, keepdims=True)
            hn_ref[rows, :] = (xb * lax.rsqrt(ms + EPS) * g_ref[...]).astype(BF16)
            return carry
        lax.fori_loop(0, tm // row_chunk, body, 0)
        gates_ref[...] = _dot_nt(hn_ref[...], wg_ref[...]) + bg_ref[...]

    proj_ref[...] = (_dot_nt(hn_ref[...], w_ref[...]) + b_ref[...]).astype(proj_ref.dtype)


def _dot_nt(a, b_t):
    return lax.dot_general(a, b_t, (((1,), (1,)), ((), ())), preferred_element_type=F32)


def _inproj(x2, g, w_main_t, b_main, w_gate_t, b_gate, *, tm=1024, tn=1024):
    M, D = x2.shape
    N = w_main_t.shape[0]
    assert M % tm == 0 and N % tn == 0 and w_gate_t.shape == (LANES, D)
    return pl.pallas_call(
        functools.partial(_inproj_kernel, row_chunk=128),
        grid=(M // tm, N // tn),
        in_specs=[
            pl.BlockSpec((tm, D), lambda i, j: (i, 0)),
            pl.BlockSpec((1, D), lambda i, j: (0, 0)),
            pl.BlockSpec((tn, D), lambda i, j: (j, 0)),
            pl.BlockSpec((1, tn), lambda i, j: (0, j)),
            pl.BlockSpec((LANES, D), lambda i, j: (0, 0)),
            pl.BlockSpec((1, LANES), lambda i, j: (0, 0)),
        ],
        out_specs=[
            pl.BlockSpec((tm, tn), lambda i, j: (i, j)),
            pl.BlockSpec((tm, LANES), lambda i, j: (i, 0)),
        ],
        out_shape=[
            jax.ShapeDtypeStruct((M, N), BF16),
            jax.ShapeDtypeStruct((M, LANES), F32),
        ],
        scratch_shapes=[pltpu.VMEM((tm, D), BF16)],
        compiler_params=_params(2),
        name="inproj",
    )(x2, g, w_main_t, b_main, w_gate_t, b_gate)


def _split3(a):
    hi = a.astype(BF16)
    r1 = a - hi.astype(F32)
    mid = r1.astype(BF16)
    lo = (r1 - mid.astype(F32)).astype(BF16)
    return hi, mid, lo


def _mlstm_kernel(p_ref, g_ref, hg_ref, out_ref, st_ref, m_ref):
    c = pl.program_id(1)
    L = p_ref.shape[0]
    ext = M_DV + LANES

    @pl.when(c == 0)
    def _():
        st_ref[...] = jnp.zeros_like(st_ref)
        m_ref[...] = jnp.zeros_like(m_ref)

    X = g_ref[...]
    lane = lax.broadcasted_iota(jnp.int32, X.shape, 1)
    logsig = jnp.minimum(X, 0.0) - jnp.log1p(jnp.exp(-jnp.abs(X)))
    Xl = jnp.where((lane >= M_HEADS) & (lane < GATE_COLS), logsig, X)

    row = lax.broadcasted_iota(jnp.int32, (L, L), 0)
    col = lax.broadcasted_iota(jnp.int32, (L, L), 1)
    causal = row >= col
    tri = causal.astype(BF16)
    parts = _split3(Xl)
    Bc = (jnp.dot(tri, parts[2], preferred_element_type=F32)
          + jnp.dot(tri, parts[1], preferred_element_type=F32)
          + jnp.dot(tri, parts[0], preferred_element_type=F32))
    XT = Xl.T
    BT = Bc.T

    ones_blk = jnp.ones((L, LANES), BF16)
    for h in range(M_HEADS):
        i_col = Xl[:, h:h + 1]
        b_col = Bc[:, M_HEADS + h:M_HEADS + h + 1]
        i_row = XT[h:h + 1, :]
        b_row = BT[M_HEADS + h:M_HEADS + h + 1, :]
        m_prev = m_ref[h][0:1, 0:1]

        q = p_ref[:, h * M_DQK:(h + 1) * M_DQK]
        k = p_ref[:, M_QK_WIDTH + h * M_DQK:M_QK_WIDTH + (h + 1) * M_DQK]
        v = p_ref[:, 2 * M_QK_WIDTH + h * M_DV:2 * M_QK_WIDTH + (h + 1) * M_DV]
        o_pre = p_ref[:, 2 * M_QK_WIDTH + M_WIDTH + h * M_DV:
                      2 * M_QK_WIDTH + M_WIDTH + (h + 1) * M_DV].astype(F32)
        v_ext = jnp.concatenate([v, ones_blk], axis=1)

        Dm = jnp.where(causal, b_col - (b_row - i_row), -jnp.inf)
        m_inter = b_col + m_prev
        m_t = jnp.maximum(jnp.max(Dm, axis=-1, keepdims=True), m_inter)
        qk = lax.dot_general(q, k, (((1,), (1,)), ((), ())), preferred_element_type=F32)
        S = (qk * (QK_SCALE * jnp.exp(Dm - m_t))).astype(BF16)
        intra = jnp.dot(S, v_ext, preferred_element_type=F32)
        Ct = st_ref[h]
        inter = jnp.dot(q, Ct.astype(BF16), preferred_element_type=F32)
        tot = intra + (QK_SCALE * jnp.exp(m_inter - m_t)) * inter
        num = tot[:, :M_DV]
        den = tot[:, M_DV:M_DV + 1]
        hh = num / jnp.maximum(jnp.abs(den), jnp.exp(-m_t))

        hh = hh * lax.rsqrt(jnp.mean(hh * hh, axis=-1, keepdims=True) + EPS)
        hh = hh * hg_ref[:, h * M_DV:(h + 1) * M_DV]
        out_ref[:, h * M_DV:(h + 1) * M_DV] = (hh * _sigmoid(o_pre)).astype(out_ref.dtype)

        a = b_col[L - 1:L, :]
        w = a - b_col + i_col
        m_new = jnp.maximum(a + m_prev, jnp.max(w, axis=0, keepdims=True))
        kw = (k.astype(F32) * jnp.exp(w - m_new)).astype(BF16)
        upd = lax.dot_general(kw, v_ext, (((0,), (0,)), ((), ())), preferred_element_type=F32)
        st_ref[h] = jnp.exp(a + m_prev - m_new) * Ct + upd
        m_ref[h] = jnp.broadcast_to(m_new, m_ref.shape[1:])


def _mlstm(proj, gates, head_g, *, batch, seq, chunk=256):
    M = proj.shape[0]
    nc = seq // chunk
    assert seq % chunk == 0 and M == batch * seq
    width = 2 * M_QK_WIDTH + 2 * M_WIDTH
    return pl.pallas_call(
        _mlstm_kernel,
        grid=(batch, nc),
        in_specs=[
            pl.BlockSpec((chunk, width), lambda b, c: (b * nc + c, 0)),
            pl.BlockSpec((chunk, LANES), lambda b, c: (b * nc + c, 0)),
            pl.BlockSpec((1, M_WIDTH), lambda b, c: (0, 0)),
        ],
        out_specs=pl.BlockSpec((chunk, M_WIDTH), lambda b, c: (b * nc + c, 0)),
        out_shape=jax.ShapeDtypeStruct((M, M_WIDTH), BF16),
        scratch_shapes=[
            pltpu.VMEM((M_HEADS, M_DQK, M_DV + LANES), F32),
            pltpu.VMEM((M_HEADS, 8, LANES), F32),
        ],
        compiler_params=_params(2),
        name="mlstm",
    )(proj, gates, head_g)


def _causal_conv3(cur, prev2, prev1, w_ref):
    row = lax.broadcasted_iota(jnp.int32, cur.shape, 0)
    r1 = jnp.where(row == 0, prev1, pltpu.roll(cur, 1, 0))
    r2 = jnp.where(row == 0, prev2, jnp.where(row == 1, prev1, pltpu.roll(cur, 2, 0)))
    return r2 * w_ref[0:1, :] + r1 * w_ref[1:2, :] + cur * w_ref[2:3, :]


def _mix_kernel(hm_ref, cb_ref, halo_ref, gm_ref, gc_ref, x_ref, cw_ref, wbm_ref, wbc_ref,
                wout_ref, g2_ref, h1_ref, hn2_ref, merged_ref, *, blocks_per_seq, col_chunk):
    i = pl.program_id(0)
    tm = hm_ref.shape[0]
    D = x_ref.shape[1]

    cb = cb_ref[:, 0:C_WIDTH].astype(F32)
    pp = cb_ref[:, C_WIDTH:2 * C_WIDTH].astype(F32) * cb_ref[:, 2 * C_WIDTH:3 * C_WIDTH].astype(F32)
    hp = halo_ref[:, 0:C_WIDTH].astype(F32) * halo_ref[:, C_WIDTH:2 * C_WIDTH].astype(F32)
    hp = jnp.where(i % blocks_per_seq != 0, hp, 0.0)
    nh = halo_ref.shape[0]
    z = (cb * _causal_conv3(pp, hp[nh - 2:nh - 1, :], hp[nh - 1:nh, :], cw_ref)).astype(BF16)
    hm = hm_ref[...]

    for n in range(D // col_chunk):
        cols = slice(n * col_chunk, (n + 1) * col_chunk)
        y_m = jnp.dot(hm, wbm_ref[:, cols], preferred_element_type=F32)
        y_c = jnp.dot(z, wbc_ref[:, cols], preferred_element_type=F32)
        merged_ref[:, cols] = (_sigmoid(gm_ref[:, cols].astype(F32)) * y_m
                               + _sigmoid(gc_ref[:, cols].astype(F32)) * y_c).astype(BF16)

    ssq = jnp.zeros((tm, 1), F32)
    for n in range(D // col_chunk):
        cols = slice(n * col_chunk, (n + 1) * col_chunk)
        h1 = x_ref[:, cols] + jnp.dot(merged_ref[...], wout_ref[:, cols],
                                      preferred_element_type=F32)
        h1_ref[:, cols] = h1
        ssq = ssq + jnp.sum(h1 * h1, axis=-1, keepdims=True)
    rs = lax.rsqrt(ssq / D + EPS)
    for n in range(D // col_chunk):
        cols = slice(n * col_chunk, (n + 1) * col_chunk)
        hn2_ref[:, cols] = (h1_ref[:, cols] * rs * g2_ref[:, cols]).astype(hn2_ref.dtype)


def _mix(hm, proj, x2, conv_w, w_bm, w_bc, w_out, g2, *, seq, tm=256, halo=16):
    M, D = x2.shape
    assert M % tm == 0 and seq % tm == 0 and tm % halo == 0
    cb_blk = MAIN_SPLIT // (3 * C_WIDTH)
    assert cb_blk * 3 * C_WIDTH == MAIN_SPLIT
    cc_off = MAIN_SPLIT + C_WIDTH
    gm_off = MAIN_SPLIT + 3 * C_WIDTH
    assert cc_off % (2 * C_WIDTH) == 0 and gm_off % D == 0
    const = dict(pipeline_mode=pl.Buffered(1))
    return pl.pallas_call(
        functools.partial(_mix_kernel, blocks_per_seq=seq // tm, col_chunk=512),
        grid=(M // tm,),
        in_specs=[
            pl.BlockSpec((tm, M_WIDTH), lambda i: (i, 0)),
            pl.BlockSpec((tm, 3 * C_WIDTH), lambda i: (i, cb_blk)),
            pl.BlockSpec((halo, 2 * C_WIDTH),
                         lambda i: (jnp.maximum(i * (tm // halo) - 1, 0), cc_off // (2 * C_WIDTH))),
            pl.BlockSpec((tm, D), lambda i: (i, gm_off // D)),
            pl.BlockSpec((tm, D), lambda i: (i, gm_off // D + 1)),
            pl.BlockSpec((tm, D), lambda i: (i, 0)),
            pl.BlockSpec(conv_w.shape, lambda i: (0, 0)),
            pl.BlockSpec(w_bm.shape, lambda i: (0, 0), **const),
            pl.BlockSpec(w_bc.shape, lambda i: (0, 0), **const),
            pl.BlockSpec(w_out.shape, lambda i: (0, 0), **const),
            pl.BlockSpec((1, D), lambda i: (0, 0)),
        ],
        out_specs=[
            pl.BlockSpec((tm, D), lambda i: (i, 0)),
            pl.BlockSpec((tm, D), lambda i: (i, 0)),
        ],
        out_shape=[
            jax.ShapeDtypeStruct((M, D), F32),
            jax.ShapeDtypeStruct((M, D), BF16),
        ],
        scratch_shapes=[pltpu.VMEM((tm, D), BF16)],
        compiler_params=_params(1),
        name="mix",
    )(hm, proj, proj, proj, proj, x2, conv_w, w_bm, w_bc, w_out, g2)


def _ffn_kernel(hn_ref, h1_ref, wu_ref, cw_ref, cb_ref, wd_ref, g3_ref, out_ref,
                u_ref, a_ref, carry_ref, *, nj, blocks_per_seq):
    s = pl.program_id(0)
    tm, tn = hn_ref.shape[0], wd_ref.shape[0]
    tile_b = jnp.maximum(s - 1, 0)
    ib, jb = tile_b // nj, tile_b % nj
    jc = (s + nj - 2) % nj

    @pl.when(s == 0)
    def _():
        u_ref[...] = jnp.zeros_like(u_ref)
        a_ref[...] = jnp.zeros_like(a_ref)
        carry_ref[...] = jnp.zeros_like(carry_ref)

    @pl.when((jc == 0) | (s == 0))
    def _():
        out_ref[...] = jnp.zeros_like(out_ref)

    out_ref[...] += jnp.dot(a_ref[...], wd_ref[...], preferred_element_type=F32)

    u = u_ref[...]
    prev = jnp.where(ib % blocks_per_seq == 0, 0.0, carry_ref[jb])
    carry_ref[jb] = u[tm - 8:tm, :]
    y = _causal_conv3(u, prev[6:7, :], prev[7:8, :], cw_ref) + cb_ref[...]
    gate, val = y[:, :tn], y[:, tn:]
    a_ref[...] = (gate * _sigmoid(gate) * val).astype(BF16)

    u_ref[...] = jnp.dot(hn_ref[...], wu_ref[...], preferred_element_type=F32)

    @pl.when((jc == nj - 1) & (s >= 2))
    def _():
        h2 = out_ref[...] + h1_ref[...]
        rs = lax.rsqrt(jnp.mean(h2 * h2, axis=-1, keepdims=True) + EPS)
        out_ref[...] = h2 * rs * g3_ref[...]


def _ffn(hn2, h1, w_up_t, cw_t, cb_t, w_down_p, g3, *, seq, tm=512):
    M, D = h1.shape
    nj, _, tn2 = w_up_t.shape
    tn = tn2 // 2
    assert M % tm == 0 and seq % tm == 0 and w_down_p.shape[0] == nj * tn
    ni = M // tm
    n_tiles = ni * nj
    row_a = lambda s: (jnp.minimum(s // nj, ni - 1), 0)
    row_c = lambda s: (jnp.maximum(s - 2, 0) // nj, 0)
    tile_a = lambda s: (s % nj, 0, 0)
    tile_b = lambda s: (jnp.maximum(s - 1, 0) % nj, 0, 0)
    return pl.pallas_call(
        functools.partial(_ffn_kernel, nj=nj, blocks_per_seq=seq // tm),
        grid=(n_tiles + 2,),
        in_specs=[
            pl.BlockSpec((tm, D), row_a),
            pl.BlockSpec((tm, D), row_c),
            pl.BlockSpec((None, D, tn2), tile_a),
            pl.BlockSpec((None, 3, tn2), tile_b),
            pl.BlockSpec((None, 1, tn2), tile_b),
            pl.BlockSpec((tn, D), lambda s: ((s + nj - 2) % nj, 0)),
            pl.BlockSpec((1, D), lambda s: (0, 0)),
        ],
        out_specs=pl.BlockSpec((tm, D), row_c),
        out_shape=jax.ShapeDtypeStruct((M, D), F32),
        scratch_shapes=[
            pltpu.VMEM((tm, tn2), F32),
            pltpu.VMEM((tm, tn), BF16),
            pltpu.VMEM((nj, 8, tn2), F32),
        ],
        compiler_params=_params(1),
        name="ffn",
    )(hn2, h1, w_up_t, cw_t, cb_t, w_down_p, g3)


def _win_prep_kernel(w_ref, tail_ref, o_ref, *, first_shifted):
    i = pl.program_id(0)
    m = w_ref[...]
    shifted = jnp.concatenate([m[GATE_COLS:, :], tail_ref[...]], axis=0)
    o_ref[...] = jnp.where(i >= first_shifted, shifted, m).astype(BF16)


def _win_prep(w_in_t, *, rows=512):
    n, D = w_in_t.shape
    n_main = n - GATE_COLS
    assert n_main % rows == 0 and MAIN_SPLIT % rows == 0 and rows % GATE_COLS == 0
    return pl.pallas_call(
        functools.partial(_win_prep_kernel, first_shifted=MAIN_SPLIT // rows),
        grid=(n_main // rows,),
        in_specs=[
            pl.BlockSpec((rows, D), lambda i: (i, 0)),
            pl.BlockSpec((GATE_COLS, D), lambda i: ((i + 1) * (rows // GATE_COLS), 0)),
        ],
        out_specs=pl.BlockSpec((rows, D), lambda i: (i, 0)),
        out_shape=jax.ShapeDtypeStruct((n_main, D), BF16),
        compiler_params=_params(1),
        name="win_prep",
    )(w_in_t, w_in_t)


def _wdown_prep_kernel(w_ref, o_ref, *, d_ff):
    rows = w_ref.shape[0]
    row = pl.program_id(0) * rows + lax.broadcasted_iota(jnp.int32, w_ref.shape, 0)
    o_ref[...] = jnp.where(row < d_ff, w_ref[...], 0.0).astype(BF16)


def _wdown_prep(w_down, *, rows):
    d_ff, D = w_down.shape
    nb = pl.cdiv(d_ff, rows)
    return pl.pallas_call(
        functools.partial(_wdown_prep_kernel, d_ff=d_ff),
        grid=(nb,),
        in_specs=[pl.BlockSpec((rows, D), lambda i: (i, 0))],
        out_specs=pl.BlockSpec((rows, D), lambda i: (i, 0)),
        out_shape=jax.ShapeDtypeStruct((nb * rows, D), BF16),
        compiler_params=_params(1),
        name="wdown_prep",
    )(w_down)


def _wup_prep_kernel(w_ref, o_ref, *, d_ff, tn):
    nj = o_ref.shape[0]
    for j in range(nj):
        lo = j * tn
        width = min(tn, d_ff - lo)
        o_ref[j, :, 0:width] = w_ref[:, lo:lo + width].astype(BF16)
        o_ref[j, :, tn:tn + width] = w_ref[:, d_ff + lo:d_ff + lo + width].astype(BF16)
        if width < tn:
            pad = jnp.zeros((o_ref.shape[1], tn - width), BF16)
            o_ref[j, :, width:tn] = pad
            o_ref[j, :, tn + width:2 * tn] = pad


def _wup_prep(w_up, *, d_ff, tn, rows=128):
    D = w_up.shape[0]
    nj = pl.cdiv(d_ff, tn)
    assert D % rows == 0 and d_ff % LANES == 0 and tn % LANES == 0
    return pl.pallas_call(
        functools.partial(_wup_prep_kernel, d_ff=d_ff, tn=tn),
        grid=(D // rows,),
        in_specs=[pl.BlockSpec((rows, 2 * d_ff), lambda r: (r, 0))],
        out_specs=pl.BlockSpec((nj, rows, 2 * tn), lambda r: (0, r, 0)),
        out_shape=jax.ShapeDtypeStruct((nj, D, 2 * tn), BF16),
        compiler_params=_params(1),
        name="wup_prep",
    )(w_up)


def _tile_pairs(a, d_ff, tn):
    nj = pl.cdiv(d_ff, tn)
    pad = lambda h: jnp.pad(h, ((0, 0), (0, nj * tn - d_ff))).reshape(a.shape[0], nj, tn)
    both = jnp.concatenate([pad(a[:, :d_ff]), pad(a[:, d_ff:])], axis=2)
    return jnp.transpose(both, (1, 0, 2))


def kernel(x, norm_mix_g, w_in, b_in, mlstm_head_g, w_branch_m, conv_w, w_branch_c, w_out,
           norm_ffn_g, w_up, ffn_conv_w, ffn_conv_b, w_down, norm_out_g):
    B, T, D = x.shape
    M = B * T
    d_ff = w_down.shape[0]
    ff_tile = 512
    ffp = pl.cdiv(d_ff, ff_tile) * ff_tile
    x2 = x.reshape(M, D)

    g0, g1 = MAIN_SPLIT, MAIN_SPLIT + GATE_COLS
    w_in_t = w_in.T
    w_main_t = _win_prep(w_in_t)
    w_gate_t = jnp.pad(w_in_t[g0:g1], ((0, LANES - GATE_COLS), (0, 0))).astype(BF16)
    b_main = jnp.concatenate([b_in[:g0], b_in[g1:]])[None, :]
    b_gate = jnp.pad(b_in[None, g0:g1], ((0, 0), (0, LANES - GATE_COLS)))
    w_up_t = _wup_prep(w_up, d_ff=d_ff, tn=ff_tile)
    cw_t = _tile_pairs(ffn_conv_w, d_ff, ff_tile)
    cb_t = _tile_pairs(ffn_conv_b[None, :], d_ff, ff_tile)
    w_down_p = _wdown_prep(w_down, rows=ff_tile)

    proj, gates = _inproj(x2, norm_mix_g[None, :], w_main_t, b_main, w_gate_t, b_gate)
    hm = _mlstm(proj, gates, mlstm_head_g[None, :], batch=B, seq=T)
    h1, hn2 = _mix(hm, proj, x2, conv_w, w_branch_m.astype(BF16), w_branch_c.astype(BF16),
                   w_out.astype(BF16), norm_ffn_g[None, :], seq=T)
    out = _ffn(hn2, h1, w_up_t, cw_t, cb_t, w_down_p, norm_out_g[None, :], seq=T)
    return out.reshape(B, T, D)
```

```python
import functools

import jax
import jax.numpy as jnp
from jax import lax
from jax.experimental import pallas as pl
from jax.experimental.pallas import tpu as pltpu

F32 = jnp.float32
BF16 = jnp.bfloat16

EPS = 1e-6
M_HEADS = 4
M_DV = 256
M_DQK = 128
M_WIDTH = M_HEADS * M_DV
M_QK_WIDTH = M_HEADS * M_DQK
C_WIDTH = 1024
QK_SCALE = M_DQK ** -0.5

LANES = 128
GATE_COLS = 2 * M_HEADS
MAIN_SPLIT = 2 * M_QK_WIDTH + 2 * M_WIDTH

VMEM_LIMIT = 56 * 1024 * 1024


def _params(n_axes, vmem=VMEM_LIMIT, flags=None):
    return pltpu.CompilerParams(dimension_semantics=("arbitrary",) * n_axes,
                                vmem_limit_bytes=vmem, flags=flags)


def _sigmoid(v):
    return 1.0 / (1.0 + jnp.exp(-v))


def _inproj_kernel(x_ref, g_ref, w_ref, b_ref, wg_ref, bg_ref, proj_ref, gates_ref, hn_ref,
                   *, row_chunk):
    j = pl.program_id(1)
    tm = x_ref.shape[0]

    @pl.when(j == 0)
    def _():
        def body(r, carry):
            rows = pl.ds(pl.multiple_of(r * row_chunk, row_chunk), row_chunk)
            xb = x_ref[rows, :]
            ms = jnp.mean(xb * xb, axis=-1, keepdims=True)
            hn_ref[rows, :] = (xb * lax.rsqrt(ms + EPS) * g_ref[...]).astype(BF16)
            return carry
        lax.fori_loop(0, tm // row_chunk, body, 0)
        gates_ref[...] = _dot_nt(hn_ref[...], wg_ref[...]) + bg_ref[...]

    proj_ref[...] = (_dot_nt(hn_ref[...], w_ref[...]) + b_ref[...]).astype(proj_ref.dtype)


def _dot_nt(a, b_t):
    return lax.dot_general(a, b_t, (((1,), (1,)), ((), ())), preferred_element_type=F32)


def _inproj(x2, g, w_main_t, b_main, w_gate_t, b_gate, *, tm=1024, tn=1024):
    M, D = x2.shape
    N = w_main_t.shape[0]
    assert M % tm == 0 and N % tn == 0 and w_gate_t.shape == (LANES, D)
    return pl.pallas_call(
        functools.partial(_inproj_kernel, row_chunk=128),
        grid=(M // tm, N // tn),
        in_specs=[
            pl.BlockSpec((tm, D), lambda i, j: (i, 0)),
            pl.BlockSpec((1, D), lambda i, j: (0, 0)),
            pl.BlockSpec((tn, D), lambda i, j: (j, 0)),
            pl.BlockSpec((1, tn), lambda i, j: (0, j)),
            pl.BlockSpec((LANES, D), lambda i, j: (0, 0)),
            pl.BlockSpec((1, LANES), lambda i, j: (0, 0)),
        ],
        out_specs=[
            pl.BlockSpec((tm, tn), lambda i, j: (i, j)),
            pl.BlockSpec((tm, LANES), lambda i, j: (i, 0)),
        ],
        out_shape=[
            jax.ShapeDtypeStruct((M, N), BF16),
            jax.ShapeDtypeStruct((M, LANES), F32),
        ],
        scratch_shapes=[pltpu.VMEM((tm, D), BF16)],
        compiler_params=_params(2),
        name="inproj",
    )(x2, g, w_main_t, b_main, w_gate_t, b_gate)


def _split3(a):
    hi = a.astype(BF16)
    r1 = a - hi.astype(F32)
    mid = r1.astype(BF16)
    lo = (r1 - mid.astype(F32)).astype(BF16)
    return hi, mid, lo


def _mlstm_kernel(p_ref, g_ref, hg_ref, out_ref, st_ref, m_ref):
    c = pl.program_id(1)
    L = p_ref.shape[0]
    ext = M_DV + LANES

    @pl.when(c == 0)
    def _():
        st_ref[...] = jnp.zeros_like(st_ref)
        m_ref[...] = jnp.zeros_like(m_ref)

    X = g_ref[...]
    lane = lax.broadcasted_iota(jnp.int32, X.shape, 1)
    logsig = jnp.minimum(X, 0.0) - jnp.log1p(jnp.exp(-jnp.abs(X)))
    Xl = jnp.where((lane >= M_HEADS) & (lane < GATE_COLS), logsig, X)

    row = lax.broadcasted_iota(jnp.int32, (L, L), 0)
    col = lax.broadcasted_iota(jnp.int32, (L, L), 1)
    causal = row >= col
    tri = causal.astype(BF16)
    parts = _split3(Xl)
    Bc = (jnp.dot(tri, parts[2], preferred_element_type=F32)
          + jnp.dot(tri, parts[1], preferred_element_type=F32)
          + jnp.dot(tri, parts[0], preferred_element_type=F32))
    XT = Xl.T
    BT = Bc.T

    ones_blk = jnp.ones((L, LANES), BF16)
    for h in range(M_HEADS):
        i_col = Xl[:, h:h + 1]
        b_col = Bc[:, M_HEADS + h:M_HEADS + h + 1]
        i_row = XT[h:h + 1, :]
        b_row = BT[M_HEADS + h:M_HEADS + h + 1, :]
        m_prev = m_ref[h][0:1, 0:1]

        q = p_ref[:, h * M_DQK:(h + 1) * M_DQK]
        k = p_ref[:, M_QK_WIDTH + h * M_DQK:M_QK_WIDTH + (h + 1) * M_DQK]
        v = p_ref[:, 2 * M_QK_WIDTH + h * M_DV:2 * M_QK_WIDTH + (h + 1) * M_DV]
        o_pre = p_ref[:, 2 * M_QK_WIDTH + M_WIDTH + h * M_DV:
                      2 * M_QK_WIDTH + M_WIDTH + (h + 1) * M_DV].astype(F32)
        v_ext = jnp.concatenate([v, ones_blk], axis=1)

        Dm = jnp.where(causal, b_col - (b_row - i_row), -jnp.inf)
        m_inter = b_col + m_prev
        m_t = jnp.maximum(jnp.max(Dm, axis=-1, keepdims=True), m_inter)
        qk = lax.dot_general(q, k, (((1,), (1,)), ((), ())), preferred_element_type=F32)
        S = (qk * (QK_SCALE * jnp.exp(Dm - m_t))).astype(BF16)
        intra = jnp.dot(S, v_ext, preferred_element_type=F32)
        Ct = st_ref[h]
        inter = jnp.dot(q, Ct.astype(BF16), preferred_element_type=F32)
        tot = intra + (QK_SCALE * jnp.exp(m_inter - m_t)) * inter
        num = tot[:, :M_DV]
        den = tot[:, M_DV:M_DV + 1]
        hh = num / jnp.maximum(jnp.abs(den), jnp.exp(-m_t))

        hh = hh * lax.rsqrt(jnp.mean(hh * hh, axis=-1, keepdims=True) + EPS)
        hh = hh * hg_ref[:, h * M_DV:(h + 1) * M_DV]
        out_ref[:, h * M_DV:(h + 1) * M_DV] = (hh * _sigmoid(o_pre)).astype(out_ref.dtype)

        a = b_col[L - 1:L, :]
        w = a - b_col + i_col
        m_new = jnp.maximum(a + m_prev, jnp.max(w, axis=0, keepdims=True))
        kw = (k.astype(F32) * jnp.exp(w - m_new)).astype(BF16)
        upd = lax.dot_general(kw, v_ext, (((0,), (0,)), ((), ())), preferred_element_type=F32)
        st_ref[h] = jnp.exp(a + m_prev - m_new) * Ct + upd
        m_ref[h] = jnp.broadcast_to(m_new, m_ref.shape[1:])


def _mlstm(proj, gates, head_g, *, batch, seq, chunk=256):
    M = proj.shape[0]
    nc = seq // chunk
    assert seq % chunk == 0 and M == batch * seq
    width = 2 * M_QK_WIDTH + 2 * M_WIDTH
    return pl.pallas_call(
        _mlstm_kernel,
        grid=(batch, nc),
        in_specs=[
            pl.BlockSpec((chunk, width), lambda b, c: (b * nc + c, 0)),
            pl.BlockSpec((chunk, LANES), lambda b, c: (b * nc + c, 0)),
            pl.BlockSpec((1, M_WIDTH), lambda b, c: (0, 0)),
        ],
        out_specs=pl.BlockSpec((chunk, M_WIDTH), lambda b, c: (b * nc + c, 0)),
        out_shape=jax.ShapeDtypeStruct((M, M_WIDTH), BF16),
        scratch_shapes=[
            pltpu.VMEM((M_HEADS, M_DQK, M_DV + LANES), F32),
            pltpu.VMEM((M_HEADS, 8, LANES), F32),
        ],
        compiler_params=_params(2),
        name="mlstm",
    )(proj, gates, head_g)


def _causal_conv3(cur, prev2, prev1, w_ref):
    row = lax.broadcasted_iota(jnp.int32, cur.shape, 0)
    r1 = jnp.where(row == 0, prev1, pltpu.roll(cur, 1, 0))
    r2 = jnp.where(row == 0, prev2, jnp.where(row == 1, prev1, pltpu.roll(cur, 2, 0)))
    return r2 * w_ref[0:1, :] + r1 * w_ref[1:2, :] + cur * w_ref[2:3, :]


def _mix_kernel(hm_ref, cb_ref, halo_ref, gm_ref, gc_ref, x_ref, cw_ref, wbm_ref, wbc_ref,
                wout_ref, g2_ref, h1_ref, hn2_ref, merged_ref, *, blocks_per_seq, col_chunk):
    i = pl.program_id(0)
    tm = hm_ref.shape[0]
    D = x_ref.shape[1]

    cb = cb_ref[:, 0:C_WIDTH].astype(F32)
    pp = cb_ref[:, C_WIDTH:2 * C_WIDTH].astype(F32) * cb_ref[:, 2 * C_WIDTH:3 * C_WIDTH].astype(F32)
    hp = halo_ref[:, 0:C_WIDTH].astype(F32) * halo_ref[:, C_WIDTH:2 * C_WIDTH].astype(F32)
    hp = jnp.where(i % blocks_per_seq != 0, hp, 0.0)
    nh = halo_ref.shape[0]
    z = (cb * _causal_conv3(pp, hp[nh - 2:nh - 1, :], hp[nh - 1:nh, :], cw_ref)).astype(BF16)
    hm = hm_ref[...]

    for n in range(D // col_chunk):
        cols = slice(n * col_chunk, (n + 1) * col_chunk)
        y_m = jnp.dot(hm, wbm_ref[:, cols], preferred_element_type=F32)
        y_c = jnp.dot(z, wbc_ref[:, cols], preferred_element_type=F32)
        merged_ref[:, cols] = (_sigmoid(gm_ref[:, cols].astype(F32)) * y_m
                               + _sigmoid(gc_ref[:, cols].astype(F32)) * y_c).astype(BF16)

    ssq = jnp.zeros((tm, 1), F32)
    for n in range(D // col_chunk):
        cols = slice(n * col_chunk, (n + 1) * col_chunk)
        h1 = x_ref[:, cols] + jnp.dot(merged_ref[...], wout_ref[:, cols],
                                      preferred_element_type=F32)
        h1_ref[:, cols] = h1
        ssq = ssq + jnp.sum(h1 * h1, axis=-1, keepdims=True)
    rs = lax.rsqrt(ssq / D + EPS)
    for n in range(D // col_chunk):
        cols = slice(n * col_chunk, (n + 1) * col_chunk)
        hn2_ref[:, cols] = (h1_ref[:, cols] * rs * g2_ref[:, cols]).astype(hn2_ref.dtype)


def _mix(hm, proj, x2, conv_w, w_bm, w_bc, w_out, g2, *, seq, tm=256, halo=16):
    M, D = x2.shape
    assert M % tm == 0 and seq % tm == 0 and tm % halo == 0
    cb_blk = MAIN_SPLIT // (3 * C_WIDTH)
    assert cb_blk * 3 * C_WIDTH == MAIN_SPLIT
    cc_off = MAIN_SPLIT + C_WIDTH
    gm_off = MAIN_SPLIT + 3 * C_WIDTH
    assert cc_off % (2 * C_WIDTH) == 0 and gm_off % D == 0
    const = dict(pipeline_mode=pl.Buffered(1))
    return pl.pallas_call(
        functools.partial(_mix_kernel, blocks_per_seq=seq // tm, col_chunk=512),
        grid=(M // tm,),
        in_specs=[
            pl.BlockSpec((tm, M_WIDTH), lambda i: (i, 0)),
            pl.BlockSpec((tm, 3 * C_WIDTH), lambda i: (i, cb_blk)),
            pl.BlockSpec((halo, 2 * C_WIDTH),
                         lambda i: (jnp.maximum(i * (tm // halo) - 1, 0), cc_off // (2 * C_WIDTH))),
            pl.BlockSpec((tm, D), lambda i: (i, gm_off // D)),
            pl.BlockSpec((tm, D), lambda i: (i, gm_off // D + 1)),
            pl.BlockSpec((tm, D), lambda i: (i, 0)),
            pl.BlockSpec(conv_w.shape, lambda i: (0, 0)),
            pl.BlockSpec(w_bm.shape, lambda i: (0, 0), **const),
            pl.BlockSpec(w_bc.shape, lambda i: (0, 0), **const),
            pl.BlockSpec(w_out.shape, lambda i: (0, 0), **const),
            pl.BlockSpec((1, D), lambda i: (0, 0)),
        ],
        out_specs=[
            pl.BlockSpec((tm, D), lambda i: (i, 0)),
            pl.BlockSpec((tm, D), lambda i: (i, 0)),
        ],
        out_shape=[
            jax.ShapeDtypeStruct((M, D), F32),
            jax.ShapeDtypeStruct((M, D), BF16),
        ],
        scratch_shapes=[pltpu.VMEM((tm, D), BF16)],
        compiler_params=_params(1),
        name="mix",
    )(hm, proj, proj, proj, proj, x2, conv_w, w_bm, w_bc, w_out, g2)


HALO = 8


def _ffn_kernel(hn_ref, h1_ref, wu_ref, cw_ref, cb_ref, wd_ref, g3_ref, out_ref,
                u_ref, a_ref, carry_ref, *, nj, blocks_per_seq, norm_chunk=64, sub=512):
    s = pl.program_id(0)
    tm, tn = hn_ref.shape[0], wd_ref.shape[0]
    tile_b = jnp.maximum(s - 1, 0)
    ib, jb = tile_b // nj, tile_b % nj
    jc = (s + nj - 2) % nj

    @pl.when(s == 0)
    def _():
        u_ref[...] = jnp.zeros_like(u_ref)
        a_ref[...] = jnp.zeros_like(a_ref)
        carry_ref[...] = jnp.zeros_like(carry_ref)

    @pl.when((jc == 0) | (s == 0))
    def _():
        out_ref[...] = jnp.zeros_like(out_ref)

    prevs = [jnp.where(ib % blocks_per_seq == 0, 0.0, carry_ref[jb])]
    prevs += [u_ref[h * sub - HALO:h * sub, :] for h in range(1, tm // sub)]
    carry_ref[jb] = u_ref[tm - HALO:tm, :]

    for h in range(tm // sub):
        rows = slice(h * sub, (h + 1) * sub)
        out_ref[rows, :] += jnp.dot(a_ref[rows, :], wd_ref[...], preferred_element_type=F32)
        prev = prevs[h]
        y = _causal_conv3(u_ref[rows, :], prev[HALO - 2:HALO - 1, :], prev[HALO - 1:HALO, :],
                          cw_ref) + cb_ref[...]
        gate, val = y[:, :tn], y[:, tn:]
        a_ref[rows, :] = (gate * _sigmoid(gate) * val).astype(BF16)
        u_ref[rows, :] = jnp.dot(hn_ref[rows, :], wu_ref[...], preferred_element_type=F32)

    @pl.when((jc == nj - 1) & (s >= 2))
    def _():
        def body(r, carry):
            rows = pl.ds(pl.multiple_of(r * norm_chunk, norm_chunk), norm_chunk)
            h2 = out_ref[rows, :] + h1_ref[rows, :]
            rs = lax.rsqrt(jnp.mean(h2 * h2, axis=-1, keepdims=True) + EPS)
            out_ref[rows, :] = h2 * rs * g3_ref[...]
            return carry
        lax.fori_loop(0, tm // norm_chunk, body, 0)


def _ffn(hn2, h1, w_up_t, cw_t, cb_t, w_down_p, g3, *, seq, tm=1024):
    M, D = h1.shape
    nj, _, tn2 = w_up_t.shape
    tn = tn2 // 2
    assert M % tm == 0 and seq % tm == 0 and w_down_p.shape[0] == nj * tn
    ni = M // tm
    n_tiles = ni * nj
    row_a = lambda s: (jnp.minimum(s // nj, ni - 1), 0)
    row_c = lambda s: (jnp.maximum(s - 2, 0) // nj, 0)
    tile_a = lambda s: (s % nj, 0, 0)
    tile_b = lambda s: (jnp.maximum(s - 1, 0) % nj, 0, 0)
    return pl.pallas_call(
        functools.partial(_ffn_kernel, nj=nj, blocks_per_seq=seq // tm),
        grid=(n_tiles + 2,),
        in_specs=[
            pl.BlockSpec((tm, D), row_a),
            pl.BlockSpec((tm, D), row_c, pipeline_mode=pl.Buffered(1)),
            pl.BlockSpec((None, D, tn2), tile_a),
            pl.BlockSpec((None, 3, tn2), tile_b),
            pl.BlockSpec((None, 1, tn2), tile_b),
            pl.BlockSpec((tn, D), lambda s: ((s + nj - 2) % nj, 0)),
            pl.BlockSpec((1, D), lambda s: (0, 0)),
        ],
        out_specs=pl.BlockSpec((tm, D), row_c),
        out_shape=jax.ShapeDtypeStruct((M, D), F32),
        scratch_shapes=[
            pltpu.VMEM((tm, tn2), F32),
            pltpu.VMEM((tm, tn), BF16),
            pltpu.VMEM((nj, HALO, tn2), F32),
        ],
        compiler_params=_params(1),
        name="ffn",
    )(hn2, h1, w_up_t, cw_t, cb_t, w_down_p, g3)


def _win_prep_kernel(w_ref, tail_ref, o_ref, *, first_shifted):
    i = pl.program_id(0)
    m = w_ref[...]
    shifted = jnp.concatenate([m[GATE_COLS:, :], tail_ref[...]], axis=0)
    o_ref[...] = jnp.where(i >= first_shifted, shifted, m).astype(BF16)


def _win_prep(w_in_t, *, rows=512):
    n, D = w_in_t.shape
    n_main = n - GATE_COLS
    assert n_main % rows == 0 and MAIN_SPLIT % rows == 0 and rows % GATE_COLS == 0
    return pl.pallas_call(
        functools.partial(_win_prep_kernel, first_shifted=MAIN_SPLIT // rows),
        grid=(n_main // rows,),
        in_specs=[
            pl.BlockSpec((rows, D), lambda i: (i, 0)),
            pl.BlockSpec((GATE_COLS, D), lambda i: ((i + 1) * (rows // GATE_COLS), 0)),
        ],
        out_specs=pl.BlockSpec((rows, D), lambda i: (i, 0)),
        out_shape=jax.ShapeDtypeStruct((n_main, D), BF16),
        compiler_params=_params(1),
        name="win_prep",
    )(w_in_t, w_in_t)


def _wdown_prep_kernel(w_ref, o_ref, *, d_ff):
    rows = w_ref.shape[0]
    row = pl.program_id(0) * rows + lax.broadcasted_iota(jnp.int32, w_ref.shape, 0)
    o_ref[...] = jnp.where(row < d_ff, w_ref[...], 0.0).astype(BF16)


def _wdown_prep(w_down, *, rows):
    d_ff, D = w_down.shape
    nb = pl.cdiv(d_ff, rows)
    return pl.pallas_call(
        functools.partial(_wdown_prep_kernel, d_ff=d_ff),
        grid=(nb,),
        in_specs=[pl.BlockSpec((rows, D), lambda i: (i, 0))],
        out_specs=pl.BlockSpec((rows, D), lambda i: (i, 0)),
        out_shape=jax.ShapeDtypeStruct((nb * rows, D), BF16),
        compiler_params=_params(1),
        name="wdown_prep",
    )(w_down)


def _wup_prep_kernel(w_ref, o_ref, *, d_ff, tn):
    nj = o_ref.shape[0]
    for j in range(nj):
        lo = j * tn
        width = min(tn, d_ff - lo)
        o_ref[j, :, 0:width] = w_ref[:, lo:lo + width].astype(BF16)
        o_ref[j, :, tn:tn + width] = w_ref[:, d_ff + lo:d_ff + lo + width].astype(BF16)
        if width < tn:
            pad = jnp.zeros((o_ref.shape[1], tn - width), BF16)
            o_ref[j, :, width:tn] = pad
            o_ref[j, :, tn + width:2 * tn] = pad


def _wup_prep(w_up, *, d_ff, tn, rows=128):
    D = w_up.shape[0]
    nj = pl.cdiv(d_ff, tn)
    assert D % rows == 0 and d_ff % LANES == 0 and tn % LANES == 0
    return pl.pallas_call(
        functools.partial(_wup_prep_kernel, d_ff=d_ff, tn=tn),
        grid=(D // rows,),
        in_specs=[pl.BlockSpec((rows, 2 * d_ff), lambda r: (r, 0))],
        out_specs=pl.BlockSpec((nj, rows, 2 * tn), lambda r: (0, r, 0)),
        out_shape=jax.ShapeDtypeStruct((nj, D, 2 * tn), BF16),
        compiler_params=_params(1),
        name="wup_prep",
    )(w_up)


def _tile_pairs(a, d_ff, tn):
    nj = pl.cdiv(d_ff, tn)
    pad = lambda h: jnp.pad(h, ((0, 0), (0, nj * tn - d_ff))).reshape(a.shape[0], nj, tn)
    both = jnp.concatenate([pad(a[:, :d_ff]), pad(a[:, d_ff:])], axis=2)
    return jnp.transpose(both, (1, 0, 2))


def kernel(x, norm_mix_g, w_in, b_in, mlstm_head_g, w_branch_m, conv_w, w_branch_c, w_out,
           norm_ffn_g, w_up, ffn_conv_w, ffn_conv_b, w_down, norm_out_g):
    B, T, D = x.shape
    M = B * T
    d_ff = w_down.shape[0]
    ff_tile = 512
    ffp = pl.cdiv(d_ff, ff_tile) * ff_tile
    x2 = x.reshape(M, D)

    g0, g1 = MAIN_SPLIT, MAIN_SPLIT + GATE_COLS
    w_in_t = w_in.T
    w_main_t = _win_prep(w_in_t)
    w_gate_t = jnp.pad(w_in_t[g0:g1], ((0, LANES - GATE_COLS), (0, 0))).astype(BF16)
    b_main = jnp.concatenate([b_in[:g0], b_in[g1:]])[None, :]
    b_gate = jnp.pad(b_in[None, g0:g1], ((0, 0), (0, LANES - GATE_COLS)))
    w_up_t = _wup_prep(w_up, d_ff=d_ff, tn=ff_tile)
    cw_t = _tile_pairs(ffn_conv_w, d_ff, ff_tile)
    cb_t = _tile_pairs(ffn_conv_b[None, :], d_ff, ff_tile)
    w_down_p = _wdown_prep(w_down, rows=ff_tile)

    proj, gates = _inproj(x2, norm_mix_g[None, :], w_main_t, b_main, w_gate_t, b_gate)
    hm = _mlstm(proj, gates, mlstm_head_g[None, :], batch=B, seq=T)
    h1, hn2 = _mix(hm, proj, x2, conv_w, w_branch_m.astype(BF16), w_branch_c.astype(BF16),
                   w_out.astype(BF16), norm_ffn_g[None, :], seq=T)
    out = _ffn(hn2, h1, w_up_t, cw_t, cb_t, w_down_p, norm_out_g[None, :], seq=T)
    return out.reshape(B, T, D)
```

```python
import functools

import jax
import jax.numpy as jnp
from jax import lax
from jax.experimental import pallas as pl
from jax.experimental.pallas import tpu as pltpu

F32 = jnp.float32
BF16 = jnp.bfloat16

EPS = 1e-6
M_HEADS = 4
M_DV = 256
M_DQK = 128
M_WIDTH = M_HEADS * M_DV
M_QK_WIDTH = M_HEADS * M_DQK
C_WIDTH = 1024
QK_SCALE = M_DQK ** -0.5

LANES = 128
GATE_COLS = 2 * M_HEADS
MAIN_SPLIT = 2 * M_QK_WIDTH + 2 * M_WIDTH

VMEM_LIMIT = 56 * 1024 * 1024


def _params(n_axes, vmem=VMEM_LIMIT, flags=None):
    return pltpu.CompilerParams(dimension_semantics=("arbitrary",) * n_axes,
                                vmem_limit_bytes=vmem, flags=flags)


def _sigmoid(v):
    return 1.0 / (1.0 + jnp.exp(-v))


def _inproj_kernel(x_ref, g_ref, w_ref, b_ref, wg_ref, bg_ref, *rest, row_chunk, side):
    n_side = len(side)
    side_in = rest[:n_side]
    proj_ref, gates_ref = rest[n_side:n_side + 2]
    side_out = rest[n_side + 2:2 * n_side + 2]
    hn_ref = rest[-1]
    j = pl.program_id(1)
    tm = x_ref.shape[0]
    step = pl.program_id(0) * pl.num_programs(1) + j
    for (relayout, n_chunks), i_ref, o_ref in zip(side, side_in, side_out):
        relayout(i_ref, o_ref, jnp.minimum(step, n_chunks - 1))

    @pl.when(j == 0)
    def _():
        def body(r, carry):
            rows = pl.ds(pl.multiple_of(r * row_chunk, row_chunk), row_chunk)
            xb = x_ref[rows, :]
            ms = jnp.mean(xb * xb, axis=-1, keepdims=True)
            hn_ref[rows, :] = (xb * lax.rsqrt(ms + EPS) * g_ref[...]).astype(BF16)
            return carry
        lax.fori_loop(0, tm // row_chunk, body, 0)
        gates_ref[...] = _dot_nt(hn_ref[...], wg_ref[...]) + bg_ref[...]

    proj_ref[...] = (_dot_nt(hn_ref[...], w_ref[...]) + b_ref[...]).astype(proj_ref.dtype)


def _dot_nt(a, b_t):
    return lax.dot_general(a, b_t, (((1,), (1,)), ((), ())), preferred_element_type=F32)


def _cast_rows(i_ref, o_ref, chunk):
    o_ref[...] = i_ref[...].astype(BF16)


def _cast_rows_zero_tail(valid_rows):
    def relayout(i_ref, o_ref, chunk):
        row = chunk * i_ref.shape[0] + lax.broadcasted_iota(jnp.int32, i_ref.shape, 0)
        o_ref[...] = jnp.where(row < valid_rows, i_ref[...], 0.0).astype(BF16)
    return relayout


def _cast_tile_pairs(d_ff, tn):
    def relayout(i_ref, o_ref, chunk):
        for t in range(o_ref.shape[0]):
            lo = t * tn
            width = min(tn, d_ff - lo)
            o_ref[t, :, 0:width] = i_ref[:, lo:lo + width].astype(BF16)
            o_ref[t, :, tn:tn + width] = i_ref[:, d_ff + lo:d_ff + lo + width].astype(BF16)
            if width < tn:
                pad = jnp.zeros((o_ref.shape[1], tn - width), BF16)
                o_ref[t, :, width:tn] = pad
                o_ref[t, :, tn + width:2 * tn] = pad
    return relayout


def _inproj(x2, g, w_main_t, b_main, w_gate_t, b_gate, side, *, tm=1024, tn=1024):
    M, D = x2.shape
    N = w_main_t.shape[0]
    assert M % tm == 0 and N % tn == 0 and w_gate_t.shape == (LANES, D)
    nj = N // tn
    n_steps = (M // tm) * nj

    def chunk_map(n_chunks, axis, rank):
        def index_map(i, j):
            idx = [0] * rank
            idx[axis] = jnp.minimum(i * nj + j, n_chunks - 1)
            return tuple(idx)
        return index_map

    side_in_specs, side_out_specs, side_out_shapes, side_meta = [], [], [], []
    for arr, relayout, in_block, out_block, out_shape, n_in, n_out in side:
        assert n_out <= n_steps and n_in <= n_out
        side_in_specs.append(pl.BlockSpec(in_block, chunk_map(n_in, 0, 2)))
        side_out_specs.append(pl.BlockSpec(out_block, chunk_map(n_out, len(out_block) - 2,
                                                                len(out_block))))
        side_out_shapes.append(jax.ShapeDtypeStruct(out_shape, BF16))
        side_meta.append((relayout, n_out))

    outs = pl.pallas_call(
        functools.partial(_inproj_kernel, row_chunk=128, side=tuple(side_meta)),
        grid=(M // tm, nj),
        in_specs=[
            pl.BlockSpec((tm, D), lambda i, j: (i, 0)),
            pl.BlockSpec((1, D), lambda i, j: (0, 0)),
            pl.BlockSpec((tn, D), lambda i, j: (j, 0)),
            pl.BlockSpec((1, tn), lambda i, j: (0, j)),
            pl.BlockSpec((LANES, D), lambda i, j: (0, 0)),
            pl.BlockSpec((1, LANES), lambda i, j: (0, 0)),
        ] + side_in_specs,
        out_specs=[
            pl.BlockSpec((tm, tn), lambda i, j: (i, j)),
            pl.BlockSpec((tm, LANES), lambda i, j: (i, 0)),
        ] + side_out_specs,
        out_shape=[
            jax.ShapeDtypeStruct((M, N), BF16),
            jax.ShapeDtypeStruct((M, LANES), F32),
        ] + side_out_shapes,
        scratch_shapes=[pltpu.VMEM((tm, D), BF16)],
        compiler_params=_params(2),
        name="inproj",
    )(x2, g, w_main_t, b_main, w_gate_t, b_gate, *[s[0] for s in side])
    return outs[0], outs[1], outs[2:]


def _split3(a):
    hi = a.astype(BF16)
    r1 = a - hi.astype(F32)
    mid = r1.astype(BF16)
    lo = (r1 - mid.astype(F32)).astype(BF16)
    return hi, mid, lo


def _mlstm_kernel(p_ref, g_ref, hg_ref, out_ref, st_ref, m_ref):
    c = pl.program_id(1)
    L = p_ref.shape[0]
    ext = M_DV + LANES

    @pl.when(c == 0)
    def _():
        st_ref[...] = jnp.zeros_like(st_ref)
        m_ref[...] = jnp.zeros_like(m_ref)

    X = g_ref[...]
    lane = lax.broadcasted_iota(jnp.int32, X.shape, 1)
    logsig = jnp.minimum(X, 0.0) - jnp.log1p(jnp.exp(-jnp.abs(X)))
    Xl = jnp.where((lane >= M_HEADS) & (lane < GATE_COLS), logsig, X)

    row = lax.broadcasted_iota(jnp.int32, (L, L), 0)
    col = lax.broadcasted_iota(jnp.int32, (L, L), 1)
    causal = row >= col
    tri = causal.astype(BF16)
    parts = _split3(Xl)
    Bc = (jnp.dot(tri, parts[2], preferred_element_type=F32)
          + jnp.dot(tri, parts[1], preferred_element_type=F32)
          + jnp.dot(tri, parts[0], preferred_element_type=F32))
    XT = Xl.T
    BT = Bc.T

    ones_blk = jnp.ones((L, LANES), BF16)
    for h in range(M_HEADS):
        i_col = Xl[:, h:h + 1]
        b_col = Bc[:, M_HEADS + h:M_HEADS + h + 1]
        i_row = XT[h:h + 1, :]
        b_row = BT[M_HEADS + h:M_HEADS + h + 1, :]
        m_prev = m_ref[h][0:1, 0:1]

        q = p_ref[:, h * M_DQK:(h + 1) * M_DQK]
        k = p_ref[:, M_QK_WIDTH + h * M_DQK:M_QK_WIDTH + (h + 1) * M_DQK]
        v = p_ref[:, 2 * M_QK_WIDTH + h * M_DV:2 * M_QK_WIDTH + (h + 1) * M_DV]
        o_pre = p_ref[:, 2 * M_QK_WIDTH + M_WIDTH + h * M_DV:
                      2 * M_QK_WIDTH + M_WIDTH + (h + 1) * M_DV].astype(F32)
        v_ext = jnp.concatenate([v, ones_blk], axis=1)

        Dm = jnp.where(causal, b_col - (b_row - i_row), -jnp.inf)
        m_inter = b_col + m_prev
        m_t = jnp.maximum(jnp.max(Dm, axis=-1, keepdims=True), m_inter)
        qk = lax.dot_general(q, k, (((1,), (1,)), ((), ())), preferred_element_type=F32)
        S = (qk * (QK_SCALE * jnp.exp(Dm - m_t))).astype(BF16)
        intra = jnp.dot(S, v_ext, preferred_element_type=F32)
        Ct = st_ref[h]
        inter = jnp.dot(q, Ct.astype(BF16), preferred_element_type=F32)
        tot = intra + (QK_SCALE * jnp.exp(m_inter - m_t)) * inter
        num = tot[:, :M_DV]
        den = tot[:, M_DV:M_DV + 1]
        hh = num / jnp.maximum(jnp.abs(den), jnp.exp(-m_t))

        hh = hh * lax.rsqrt(jnp.mean(hh * hh, axis=-1, keepdims=True) + EPS)
        hh = hh * hg_ref[:, h * M_DV:(h + 1) * M_DV]
        out_ref[:, h * M_DV:(h + 1) * M_DV] = (hh * _sigmoid(o_pre)).astype(out_ref.dtype)

        a = b_col[L - 1:L, :]
        w = a - b_col + i_col
        m_new = jnp.maximum(a + m_prev, jnp.max(w, axis=0, keepdims=True))
        kw = (k.astype(F32) * jnp.exp(w - m_new)).astype(BF16)
        upd = lax.dot_general(kw, v_ext, (((0,), (0,)), ((), ())), preferred_element_type=F32)
        st_ref[h] = jnp.exp(a + m_prev - m_new) * Ct + upd
        m_ref[h] = jnp.broadcast_to(m_new, m_ref.shape[1:])


def _mlstm(proj, gates, head_g, *, batch, seq, chunk=256):
    M = proj.shape[0]
    nc = seq // chunk
    assert seq % chunk == 0 and M == batch * seq
    width = 2 * M_QK_WIDTH + 2 * M_WIDTH
    return pl.pallas_call(
        _mlstm_kernel,
        grid=(batch, nc),
        in_specs=[
            pl.BlockSpec((chunk, width), lambda b, c: (b * nc + c, 0)),
            pl.BlockSpec((chunk, LANES), lambda b, c: (b * nc + c, 0)),
            pl.BlockSpec((1, M_WIDTH), lambda b, c: (0, 0)),
        ],
        out_specs=pl.BlockSpec((chunk, M_WIDTH), lambda b, c: (b * nc + c, 0)),
        out_shape=jax.ShapeDtypeStruct((M, M_WIDTH), BF16),
        scratch_shapes=[
            pltpu.VMEM((M_HEADS, M_DQK, M_DV + LANES), F32),
            pltpu.VMEM((M_HEADS, 8, LANES), F32),
        ],
        compiler_params=_params(2),
        name="mlstm",
    )(proj, gates, head_g)


def _causal_conv3(cur, prev2, prev1, w_ref):
    row = lax.broadcasted_iota(jnp.int32, cur.shape, 0)
    r1 = jnp.where(row == 0, prev1, pltpu.roll(cur, 1, 0))
    r2 = jnp.where(row == 0, prev2, jnp.where(row == 1, prev1, pltpu.roll(cur, 2, 0)))
    return r2 * w_ref[0:1, :] + r1 * w_ref[1:2, :] + cur * w_ref[2:3, :]


def _mix_kernel(hm_ref, cb_ref, halo_ref, gm_ref, gc_ref, x_ref, cw_ref, wbm_ref, wbc_ref,
                wout_ref, g2_ref, h1_ref, hn2_ref, merged_ref, *, blocks_per_seq, col_chunk):
    i = pl.program_id(0)
    tm = hm_ref.shape[0]
    D = x_ref.shape[1]

    cb = cb_ref[:, 0:C_WIDTH].astype(F32)
    pp = cb_ref[:, C_WIDTH:2 * C_WIDTH].astype(F32) * cb_ref[:, 2 * C_WIDTH:3 * C_WIDTH].astype(F32)
    hp = halo_ref[:, 0:C_WIDTH].astype(F32) * halo_ref[:, C_WIDTH:2 * C_WIDTH].astype(F32)
    hp = jnp.where(i % blocks_per_seq != 0, hp, 0.0)
    nh = halo_ref.shape[0]
    z = (cb * _causal_conv3(pp, hp[nh - 2:nh - 1, :], hp[nh - 1:nh, :], cw_ref)).astype(BF16)
    hm = hm_ref[...]

    for n in range(D // col_chunk):
        cols = slice(n * col_chunk, (n + 1) * col_chunk)
        y_m = jnp.dot(hm, wbm_ref[:, cols], preferred_element_type=F32)
        y_c = jnp.dot(z, wbc_ref[:, cols], preferred_element_type=F32)
        merged_ref[:, cols] = (_sigmoid(gm_ref[:, cols].astype(F32)) * y_m
                               + _sigmoid(gc_ref[:, cols].astype(F32)) * y_c).astype(BF16)

    ssq = jnp.zeros((tm, 1), F32)
    for n in range(D // col_chunk):
        cols = slice(n * col_chunk, (n + 1) * col_chunk)
        h1 = x_ref[:, cols] + jnp.dot(merged_ref[...], wout_ref[:, cols],
                                      preferred_element_type=F32)
        h1_ref[:, cols] = h1
        ssq = ssq + jnp.sum(h1 * h1, axis=-1, keepdims=True)
    rs = lax.rsqrt(ssq / D + EPS)
    for n in range(D // col_chunk):
        cols = slice(n * col_chunk, (n + 1) * col_chunk)
        hn2_ref[:, cols] = (h1_ref[:, cols] * rs * g2_ref[:, cols]).astype(hn2_ref.dtype)


def _mix(hm, proj, x2, conv_w, w_bm, w_bc, w_out, g2, *, seq, tm=256, halo=16):
    M, D = x2.shape
    assert M % tm == 0 and seq % tm == 0 and tm % halo == 0
    cb_blk = MAIN_SPLIT // (3 * C_WIDTH)
    assert cb_blk * 3 * C_WIDTH == MAIN_SPLIT
    cc_off = MAIN_SPLIT + C_WIDTH
    gm_off = MAIN_SPLIT + 3 * C_WIDTH
    assert cc_off % (2 * C_WIDTH) == 0 and gm_off % D == 0
    const = dict(pipeline_mode=pl.Buffered(1))
    return pl.pallas_call(
        functools.partial(_mix_kernel, blocks_per_seq=seq // tm, col_chunk=512),
        grid=(M // tm,),
        in_specs=[
            pl.BlockSpec((tm, M_WIDTH), lambda i: (i, 0)),
            pl.BlockSpec((tm, 3 * C_WIDTH), lambda i: (i, cb_blk)),
            pl.BlockSpec((halo, 2 * C_WIDTH),
                         lambda i: (jnp.maximum(i * (tm // halo) - 1, 0), cc_off // (2 * C_WIDTH))),
            pl.BlockSpec((tm, D), lambda i: (i, gm_off // D)),
            pl.BlockSpec((tm, D), lambda i: (i, gm_off // D + 1)),
            pl.BlockSpec((tm, D), lambda i: (i, 0)),
            pl.BlockSpec(conv_w.shape, lambda i: (0, 0)),
            pl.BlockSpec(w_bm.shape, lambda i: (0, 0), **const),
            pl.BlockSpec(w_bc.shape, lambda i: (0, 0), **const),
            pl.BlockSpec(w_out.shape, lambda i: (0, 0), **const),
            pl.BlockSpec((1, D), lambda i: (0, 0)),
        ],
        out_specs=[
            pl.BlockSpec((tm, D), lambda i: (i, 0)),
            pl.BlockSpec((tm, D), lambda i: (i, 0)),
        ],
        out_shape=[
            jax.ShapeDtypeStruct((M, D), F32),
            jax.ShapeDtypeStruct((M, D), BF16),
        ],
        scratch_shapes=[pltpu.VMEM((tm, D), BF16)],
        compiler_params=_params(1),
        name="mix",
    )(hm, proj, proj, proj, proj, x2, conv_w, w_bm, w_bc, w_out, g2)


HALO = 8


def _ffn_kernel(hn_ref, h1_ref, wu_ref, cw_ref, cb_ref, wd_ref, g3_ref, out_ref,
                u_ref, a_ref, carry_ref, *, nj, blocks_per_seq, norm_chunk=64, sub=512):
    s = pl.program_id(0)
    tm, tn = hn_ref.shape[0], wd_ref.shape[0]
    tile_b = jnp.maximum(s - 1, 0)
    ib, jb = tile_b // nj, tile_b % nj
    jc = (s + nj - 2) % nj

    @pl.when(s == 0)
    def _():
        u_ref[...] = jnp.zeros_like(u_ref)
        a_ref[...] = jnp.zeros_like(a_ref)
        carry_ref[...] = jnp.zeros_like(carry_ref)

    @pl.when((jc == 0) | (s == 0))
    def _():
        out_ref[...] = jnp.zeros_like(out_ref)

    prevs = [jnp.where(ib % blocks_per_seq == 0, 0.0, carry_ref[jb])]
    prevs += [u_ref[h * sub - HALO:h * sub, :] for h in range(1, tm // sub)]
    carry_ref[jb] = u_ref[tm - HALO:tm, :]

    for h in range(tm // sub):
        rows = slice(h * sub, (h + 1) * sub)
        out_ref[rows, :] += jnp.dot(a_ref[rows, :], wd_ref[...], preferred_element_type=F32)
        prev = prevs[h]
        y = _causal_conv3(u_ref[rows, :], prev[HALO - 2:HALO - 1, :], prev[HALO - 1:HALO, :],
                          cw_ref) + cb_ref[...]
        gate, val = y[:, :tn], y[:, tn:]
        a_ref[rows, :] = (gate * _sigmoid(gate) * val).astype(BF16)
        u_ref[rows, :] = jnp.dot(hn_ref[rows, :], wu_ref[...], preferred_element_type=F32)

    @pl.when((jc == nj - 1) & (s >= 2))
    def _():
        def body(r, carry):
            rows = pl.ds(pl.multiple_of(r * norm_chunk, norm_chunk), norm_chunk)
            h2 = out_ref[rows, :] + h1_ref[rows, :]
            rs = lax.rsqrt(jnp.mean(h2 * h2, axis=-1, keepdims=True) + EPS)
            out_ref[rows, :] = h2 * rs * g3_ref[...]
            return carry
        lax.fori_loop(0, tm // norm_chunk, body, 0)


def _ffn(hn2, h1, w_up_t, cw_t, cb_t, w_down_p, g3, *, seq, tm=512):
    M, D = h1.shape
    nj, _, tn2 = w_up_t.shape
    tn = tn2 // 2
    assert M % tm == 0 and seq % tm == 0 and w_down_p.shape[0] == nj * tn
    ni = M // tm
    n_tiles = ni * nj
    row_a = lambda s: (jnp.minimum(s // nj, ni - 1), 0)
    row_c = lambda s: (jnp.maximum(s - 2, 0) // nj, 0)
    tile_a = lambda s: (s % nj, 0, 0)
    tile_b = lambda s: (jnp.maximum(s - 1, 0) % nj, 0, 0)
    return pl.pallas_call(
        functools.partial(_ffn_kernel, nj=nj, blocks_per_seq=seq // tm),
        grid=(n_tiles + 2,),
        in_specs=[
            pl.BlockSpec((tm, D), row_a),
            pl.BlockSpec((tm, D), row_c),
            pl.BlockSpec((None, D, tn2), tile_a),
            pl.BlockSpec((None, 3, tn2), tile_b),
            pl.BlockSpec((None, 1, tn2), tile_b),
            pl.BlockSpec((tn, D), lambda s: ((s + nj - 2) % nj, 0)),
            pl.BlockSpec((1, D), lambda s: (0, 0)),
        ],
        out_specs=pl.BlockSpec((tm, D), row_c),
        out_shape=jax.ShapeDtypeStruct((M, D), F32),
        scratch_shapes=[
            pltpu.VMEM((tm, tn2), F32),
            pltpu.VMEM((tm, tn), BF16),
            pltpu.VMEM((nj, HALO, tn2), F32),
        ],
        compiler_params=_params(1),
        name="ffn",
    )(hn2, h1, w_up_t, cw_t, cb_t, w_down_p, g3)


def _win_prep_kernel(w_ref, tail_ref, o_ref, *, first_shifted):
    i = pl.program_id(0)
    m = w_ref[...]
    shifted = jnp.concatenate([m[GATE_COLS:, :], tail_ref[...]], axis=0)
    o_ref[...] = jnp.where(i >= first_shifted, shifted, m).astype(BF16)


def _win_prep(w_in_t, *, rows=512):
    n, D = w_in_t.shape
    n_main = n - GATE_COLS
    assert n_main % rows == 0 and MAIN_SPLIT % rows == 0 and rows % GATE_COLS == 0
    return pl.pallas_call(
        functools.partial(_win_prep_kernel, first_shifted=MAIN_SPLIT // rows),
        grid=(n_main // rows,),
        in_specs=[
            pl.BlockSpec((rows, D), lambda i: (i, 0)),
            pl.BlockSpec((GATE_COLS, D), lambda i: ((i + 1) * (rows // GATE_COLS), 0)),
        ],
        out_specs=pl.BlockSpec((rows, D), lambda i: (i, 0)),
        out_shape=jax.ShapeDtypeStruct((n_main, D), BF16),
        compiler_params=_params(1),
        name="win_prep",
    )(w_in_t, w_in_t)


def _side_casts(w_up, w_down, w_out, w_branch_m, w_branch_c, *, d_ff, tn, chunks=64, down_rows=LANES):
    D = w_up.shape[0]
    nj = pl.cdiv(d_ff, tn)
    ffp = nj * tn
    assert d_ff % LANES == 0 and tn % LANES == 0 and ffp % down_rows == 0

    def rows_cast(w):
        r = w.shape[0] // chunks
        assert r * chunks == w.shape[0] and r % 16 == 0
        return (w, _cast_rows, (r, w.shape[1]), (r, w.shape[1]), w.shape, chunks, chunks)

    up_rows = D // chunks
    assert up_rows * chunks == D and up_rows % 16 == 0
    return [
        (w_up, _cast_tile_pairs(d_ff, tn), (up_rows, 2 * d_ff), (nj, up_rows, 2 * tn),
         (nj, D, 2 * tn), chunks, chunks),
        (w_down, _cast_rows_zero_tail(d_ff), (down_rows, D), (down_rows, D), (ffp, D),
         pl.cdiv(d_ff, down_rows), ffp // down_rows),
        rows_cast(w_out), rows_cast(w_branch_m), rows_cast(w_branch_c),
    ]


def _tile_pairs(a, d_ff, tn):
    nj = pl.cdiv(d_ff, tn)
    pad = lambda h: jnp.pad(h, ((0, 0), (0, nj * tn - d_ff))).reshape(a.shape[0], nj, tn)
    both = jnp.concatenate([pad(a[:, :d_ff]), pad(a[:, d_ff:])], axis=2)
    return jnp.transpose(both, (1, 0, 2))


def kernel(x, norm_mix_g, w_in, b_in, mlstm_head_g, w_branch_m, conv_w, w_branch_c, w_out,
           norm_ffn_g, w_up, ffn_conv_w, ffn_conv_b, w_down, norm_out_g):
    B, T, D = x.shape
    M = B * T
    d_ff = w_down.shape[0]
    ff_tile = 512
    ffp = pl.cdiv(d_ff, ff_tile) * ff_tile
    x2 = x.reshape(M, D)

    g0, g1 = MAIN_SPLIT, MAIN_SPLIT + GATE_COLS
    w_in_t = w_in.T
    w_main_t = _win_prep(w_in_t)
    w_gate_t = jnp.pad(w_in_t[g0:g1], ((0, LANES - GATE_COLS), (0, 0))).astype(BF16)
    b_main = jnp.concatenate([b_in[:g0], b_in[g1:]])[None, :]
    b_gate = jnp.pad(b_in[None, g0:g1], ((0, 0), (0, LANES - GATE_COLS)))
    cw_t = _tile_pairs(ffn_conv_w, d_ff, ff_tile)
    cb_t = _tile_pairs(ffn_conv_b[None, :], d_ff, ff_tile)

    side = _side_casts(w_up, w_down, w_out, w_branch_m, w_branch_c, d_ff=d_ff, tn=ff_tile)
    proj, gates, (w_up_t, w_down_p, w_out_b, w_bm_b, w_bc_b) = _inproj(
        x2, norm_mix_g[None, :], w_main_t, b_main, w_gate_t, b_gate, side)
    hm = _mlstm(proj, gates, mlstm_head_g[None, :], batch=B, seq=T)
    h1, hn2 = _mix(hm, proj, x2, conv_w, w_bm_b, w_bc_b, w_out_b, norm_ffn_g[None, :], seq=T)
    out = _ffn(hn2, h1, w_up_t, cw_t, cb_t, w_down_p, norm_out_g[None, :], seq=T)
    return out.reshape(B, T, D)
```

```python
import functools
import math

import jax
import jax.numpy as jnp
from jax import lax
from jax.experimental import pallas as pl
from jax.experimental.pallas import tpu as pltpu

F32 = jnp.float32
BF16 = jnp.bfloat16

EPS = 1e-6
M_HEADS = 4
M_DV = 256
M_DQK = 128
M_WIDTH = M_HEADS * M_DV
M_QK_WIDTH = M_HEADS * M_DQK
C_WIDTH = 1024
LOG_QK_SCALE = -0.5 * math.log(M_DQK)

LANES = 128
GATE_COLS = 2 * M_HEADS
MAIN_SPLIT = 2 * M_QK_WIDTH + 2 * M_WIDTH

VMEM_LIMIT = 56 * 1024 * 1024


def _params(n_axes, vmem=VMEM_LIMIT, flags=None):
    return pltpu.CompilerParams(dimension_semantics=("arbitrary",) * n_axes,
                                vmem_limit_bytes=vmem, flags=flags)


def _sigmoid(v):
    return 1.0 / (1.0 + jnp.exp(-v))


def _inproj_kernel(x_ref, g_ref, w_ref, b_ref, wg_ref, bg_ref, *rest, row_chunk, side):
    n_side = len(side)
    side_in = rest[:n_side]
    proj_ref, gates_ref = rest[n_side:n_side + 2]
    side_out = rest[n_side + 2:2 * n_side + 2]
    hn_ref = rest[-1]
    j = pl.program_id(1)
    tm = x_ref.shape[0]
    step = pl.program_id(0) * pl.num_programs(1) + j
    for (relayout, n_chunks), i_ref, o_ref in zip(side, side_in, side_out):
        relayout(i_ref, o_ref, jnp.minimum(step, n_chunks - 1))

    @pl.when(j == 0)
    def _():
        def body(r, carry):
            rows = pl.ds(pl.multiple_of(r * row_chunk, row_chunk), row_chunk)
            xb = x_ref[rows, :]
            ms = jnp.mean(xb * xb, axis=-1, keepdims=True)
            hn_ref[rows, :] = (xb * lax.rsqrt(ms + EPS) * g_ref[...]).astype(BF16)
            return carry
        lax.fori_loop(0, tm // row_chunk, body, 0)
        gates_ref[...] = _dot_nt(hn_ref[...], wg_ref[...]) + bg_ref[...]

    proj_ref[...] = (_dot_nt(hn_ref[...], w_ref[...]) + b_ref[...]).astype(proj_ref.dtype)


def _dot_nt(a, b_t):
    return lax.dot_general(a, b_t, (((1,), (1,)), ((), ())), preferred_element_type=F32)


def _cast_rows(i_ref, o_ref, chunk):
    o_ref[...] = i_ref[...].astype(BF16)


def _cast_rows_zero_tail(valid_rows):
    def relayout(i_ref, o_ref, chunk):
        row = chunk * i_ref.shape[0] + lax.broadcasted_iota(jnp.int32, i_ref.shape, 0)
        o_ref[...] = jnp.where(row < valid_rows, i_ref[...], 0.0).astype(BF16)
    return relayout


def _cast_tile_pairs(d_ff, tn):
    def relayout(i_ref, o_ref, chunk):
        for t in range(o_ref.shape[0]):
            lo = t * tn
            width = min(tn, d_ff - lo)
            o_ref[t, :, 0:width] = i_ref[:, lo:lo + width].astype(BF16)
            o_ref[t, :, tn:tn + width] = i_ref[:, d_ff + lo:d_ff + lo + width].astype(BF16)
            if width < tn:
                pad = jnp.zeros((o_ref.shape[1], tn - width), BF16)
                o_ref[t, :, width:tn] = pad
                o_ref[t, :, tn + width:2 * tn] = pad
    return relayout


def _inproj(x2, g, w_main_t, b_main, w_gate_t, b_gate, side, *, tm=1024, tn=1024):
    M, D = x2.shape
    N = w_main_t.shape[0]
    assert M % tm == 0 and N % tn == 0 and w_gate_t.shape == (2 * LANES, D)
    nj = N // tn
    n_steps = (M // tm) * nj

    def chunk_map(n_chunks, axis, rank):
        def index_map(i, j):
            idx = [0] * rank
            idx[axis] = jnp.minimum(i * nj + j, n_chunks - 1)
            return tuple(idx)
        return index_map

    side_in_specs, side_out_specs, side_out_shapes, side_meta = [], [], [], []
    for arr, relayout, in_block, out_block, out_shape, n_in, n_out in side:
        assert n_out <= n_steps and n_in <= n_out
        side_in_specs.append(pl.BlockSpec(in_block, chunk_map(n_in, 0, 2)))
        side_out_specs.append(pl.BlockSpec(out_block, chunk_map(n_out, len(out_block) - 2,
                                                                len(out_block))))
        side_out_shapes.append(jax.ShapeDtypeStruct(out_shape, BF16))
        side_meta.append((relayout, n_out))

    outs = pl.pallas_call(
        functools.partial(_inproj_kernel, row_chunk=128, side=tuple(side_meta)),
        grid=(M // tm, nj),
        in_specs=[
            pl.BlockSpec((tm, D), lambda i, j: (i, 0)),
            pl.BlockSpec((1, D), lambda i, j: (0, 0)),
            pl.BlockSpec((tn, D), lambda i, j: (j, 0)),
            pl.BlockSpec((1, tn), lambda i, j: (0, j)),
            pl.BlockSpec((2 * LANES, D), lambda i, j: (0, 0)),
            pl.BlockSpec((1, 2 * LANES), lambda i, j: (0, 0)),
        ] + side_in_specs,
        out_specs=[
            pl.BlockSpec((tm, tn), lambda i, j: (i, j)),
            pl.BlockSpec((tm, 2 * LANES), lambda i, j: (i, 0)),
        ] + side_out_specs,
        out_shape=[
            jax.ShapeDtypeStruct((M, N), BF16),
            jax.ShapeDtypeStruct((M, 2 * LANES), F32),
        ] + side_out_shapes,
        scratch_shapes=[pltpu.VMEM((tm, D), BF16)],
        compiler_params=_params(2),
        name="inproj",
    )(x2, g, w_main_t, b_main, w_gate_t, b_gate, *[s[0] for s in side])
    return outs[0], outs[1], outs[2:]


def _split3(a):
    hi = a.astype(BF16)
    r1 = a - hi.astype(F32)
    mid = r1.astype(BF16)
    lo = (r1 - mid.astype(F32)).astype(BF16)
    return hi, mid, lo


def _mlstm_kernel(p_ref, g_ref, hg_ref, out_ref, st_ref, m_ref):
    c = pl.program_id(1)
    L = p_ref.shape[0]

    @pl.when(c == 0)
    def _():
        st_ref[...] = jnp.zeros_like(st_ref)
        m_ref[...] = jnp.zeros_like(m_ref)

    logi = g_ref[:, :LANES]
    f_pre = g_ref[:, LANES:]
    logf = jnp.minimum(f_pre, 0.0) - jnp.log1p(jnp.exp(-jnp.abs(f_pre)))

    row = lax.broadcasted_iota(jnp.int32, (L, L), 0)
    col = lax.broadcasted_iota(jnp.int32, (L, L), 1)
    causal = row >= col
    tri = causal.astype(BF16)
    parts = _split3(logf)
    b = (jnp.dot(tri, parts[2], preferred_element_type=F32)
         + jnp.dot(tri, parts[1], preferred_element_type=F32)
         + jnp.dot(tri, parts[0], preferred_element_type=F32))
    g = logi - b

    trow = lax.broadcasted_iota(jnp.int32, g.shape, 0)
    run = g
    shift = 1
    while shift < L:
        run = jnp.maximum(run, jnp.where(trow >= shift, pltpu.roll(run, shift, 0), -jnp.inf))
        shift *= 2
    m_prev_row = m_ref[0:1, :]
    big_m = jnp.maximum(run, m_prev_row)
    m_last = big_m[L - 1:L, :]
    m_ref[...] = jnp.broadcast_to(b[L - 1:L, :] + m_last, m_ref.shape)
    e_col = jnp.exp(-(b + big_m))
    g_t = g.T

    ones_blk = jnp.ones((L, LANES), BF16)
    for h in range(M_HEADS):
        q = p_ref[:, h * M_DQK:(h + 1) * M_DQK]
        k = p_ref[:, M_QK_WIDTH + h * M_DQK:M_QK_WIDTH + (h + 1) * M_DQK]
        v = p_ref[:, 2 * M_QK_WIDTH + h * M_DV:2 * M_QK_WIDTH + (h + 1) * M_DV]
        o_pre = p_ref[:, 2 * M_QK_WIDTH + M_WIDTH + h * M_DV:
                      2 * M_QK_WIDTH + M_WIDTH + (h + 1) * M_DV].astype(F32)
        v_ext = jnp.concatenate([v, ones_blk], axis=1)

        g_row = g_t[h:h + 1, :]
        m_rep = jnp.broadcast_to(big_m[:, h:h + 1], (L, LANES))
        m_prev = m_prev_row[:, h:h + 1]
        m_end = m_last[:, h:h + 1]

        arg = (g_row + LOG_QK_SCALE) - jnp.concatenate([m_rep] * (L // LANES), axis=1)
        qk = lax.dot_general(q, k, (((1,), (1,)), ((), ())), preferred_element_type=F32)
        S = (qk * jnp.exp(jnp.where(causal, arg, -jnp.inf))).astype(BF16)
        intra = jnp.dot(S, v_ext, preferred_element_type=F32)
        Ct = st_ref[h]
        inter = jnp.dot(q, Ct.astype(BF16), preferred_element_type=F32)
        isc = jnp.exp((m_prev + LOG_QK_SCALE) - m_rep)
        tot = intra + jnp.concatenate([isc] * (intra.shape[1] // LANES), axis=1) * inter
        num = tot[:, :M_DV]

        den = tot[:, M_DV:M_DV + 1]
        f = 1.0 / jnp.maximum(jnp.abs(den), e_col[:, h:h + 1])
        ms = jnp.mean(num * num, axis=-1, keepdims=True)
        fac = f * lax.rsqrt(ms * (f * f) + EPS)
        hh = num * fac * hg_ref[:, h * M_DV:(h + 1) * M_DV]
        out_ref[:, h * M_DV:(h + 1) * M_DV] = (hh * _sigmoid(o_pre)).astype(out_ref.dtype)

        kw_t = (k.T.astype(F32) * jnp.exp(g_row - m_end)).astype(BF16)
        upd = jnp.dot(kw_t, v_ext, preferred_element_type=F32)
        st_ref[h] = jnp.exp(m_prev - m_end) * Ct + upd


def _mlstm(proj, gates, head_g, *, batch, seq, chunk=256):
    M = proj.shape[0]
    nc = seq // chunk
    assert seq % chunk == 0 and M == batch * seq and chunk % LANES == 0
    width = 2 * M_QK_WIDTH + 2 * M_WIDTH
    return pl.pallas_call(
        _mlstm_kernel,
        grid=(batch, nc),
        in_specs=[
            pl.BlockSpec((chunk, width), lambda b, c: (b * nc + c, 0)),
            pl.BlockSpec((chunk, 2 * LANES), lambda b, c: (b * nc + c, 0)),
            pl.BlockSpec((1, M_WIDTH), lambda b, c: (0, 0)),
        ],
        out_specs=pl.BlockSpec((chunk, M_WIDTH), lambda b, c: (b * nc + c, 0)),
        out_shape=jax.ShapeDtypeStruct((M, M_WIDTH), BF16),
        scratch_shapes=[
            pltpu.VMEM((M_HEADS, M_DQK, M_DV + LANES), F32),
            pltpu.VMEM((8, LANES), F32),
        ],
        compiler_params=_params(2),
        name="mlstm",
    )(proj, gates, head_g)


def _causal_conv3(cur, prev2, prev1, w_ref):
    row = lax.broadcasted_iota(jnp.int32, cur.shape, 0)
    r1 = jnp.where(row == 0, prev1, pltpu.roll(cur, 1, 0))
    r2 = jnp.where(row == 0, prev2, jnp.where(row == 1, prev1, pltpu.roll(cur, 2, 0)))
    return r2 * w_ref[0:1, :] + r1 * w_ref[1:2, :] + cur * w_ref[2:3, :]


def _mix_kernel(hm_ref, cb_ref, halo_ref, gm_ref, gc_ref, x_ref, cw_ref, wbm_ref, wbc_ref,
                wout_ref, g2_ref, h1_ref, hn2_ref, merged_ref, *, blocks_per_seq, col_chunk):
    i = pl.program_id(0)
    tm = hm_ref.shape[0]
    D = x_ref.shape[1]

    cb = cb_ref[:, 0:C_WIDTH].astype(F32)
    pp = cb_ref[:, C_WIDTH:2 * C_WIDTH].astype(F32) * cb_ref[:, 2 * C_WIDTH:3 * C_WIDTH].astype(F32)
    hp = halo_ref[:, 0:C_WIDTH].astype(F32) * halo_ref[:, C_WIDTH:2 * C_WIDTH].astype(F32)
    hp = jnp.where(i % blocks_per_seq != 0, hp, 0.0)
    nh = halo_ref.shape[0]
    z = (cb * _causal_conv3(pp, hp[nh - 2:nh - 1, :], hp[nh - 1:nh, :], cw_ref)).astype(BF16)
    hm = hm_ref[...]

    for n in range(D // col_chunk):
        cols = slice(n * col_chunk, (n + 1) * col_chunk)
        y_m = jnp.dot(hm, wbm_ref[:, cols], preferred_element_type=F32)
        y_c = jnp.dot(z, wbc_ref[:, cols], preferred_element_type=F32)
        merged_ref[:, cols] = (_sigmoid(gm_ref[:, cols].astype(F32)) * y_m
                               + _sigmoid(gc_ref[:, cols].astype(F32)) * y_c).astype(BF16)

    ssq = jnp.zeros((tm, 1), F32)
    for n in range(D // col_chunk):
        cols = slice(n * col_chunk, (n + 1) * col_chunk)
        h1 = x_ref[:, cols] + jnp.dot(merged_ref[...], wout_ref[:, cols],
                                      preferred_element_type=F32)
        h1_ref[:, cols] = h1
        ssq = ssq + jnp.sum(h1 * h1, axis=-1, keepdims=True)
    rs = lax.rsqrt(ssq / D + EPS)
    for n in range(D // col_chunk):
        cols = slice(n * col_chunk, (n + 1) * col_chunk)
        hn2_ref[:, cols] = (h1_ref[:, cols] * rs * g2_ref[:, cols]).astype(hn2_ref.dtype)


def _mix(hm, proj, x2, conv_w, w_bm, w_bc, w_out, g2, *, seq, tm=256, halo=16):
    M, D = x2.shape
    assert M % tm == 0 and seq % tm == 0 and tm % halo == 0
    cb_blk = MAIN_SPLIT // (3 * C_WIDTH)
    assert cb_blk * 3 * C_WIDTH == MAIN_SPLIT
    cc_off = MAIN_SPLIT + C_WIDTH
    gm_off = MAIN_SPLIT + 3 * C_WIDTH
    assert cc_off % (2 * C_WIDTH) == 0 and gm_off % D == 0
    const = dict(pipeline_mode=pl.Buffered(1))
    return pl.pallas_call(
        functools.partial(_mix_kernel, blocks_per_seq=seq // tm, col_chunk=512),
        grid=(M // tm,),
        in_specs=[
            pl.BlockSpec((tm, M_WIDTH), lambda i: (i, 0)),
            pl.BlockSpec((tm, 3 * C_WIDTH), lambda i: (i, cb_blk)),
            pl.BlockSpec((halo, 2 * C_WIDTH),
                         lambda i: (jnp.maximum(i * (tm // halo) - 1, 0), cc_off // (2 * C_WIDTH))),
            pl.BlockSpec((tm, D), lambda i: (i, gm_off // D)),
            pl.BlockSpec((tm, D), lambda i: (i, gm_off // D + 1)),
            pl.BlockSpec((tm, D), lambda i: (i, 0)),
            pl.BlockSpec(conv_w.shape, lambda i: (0, 0)),
            pl.BlockSpec(w_bm.shape, lambda i: (0, 0), **const),
            pl.BlockSpec(w_bc.shape, lambda i: (0, 0), **const),
            pl.BlockSpec(w_out.shape, lambda i: (0, 0), **const),
            pl.BlockSpec((1, D), lambda i: (0, 0)),
        ],
        out_specs=[
            pl.BlockSpec((tm, D), lambda i: (i, 0)),
            pl.BlockSpec((tm, D), lambda i: (i, 0)),
        ],
        out_shape=[
            jax.ShapeDtypeStruct((M, D), F32),
            jax.ShapeDtypeStruct((M, D), BF16),
        ],
        scratch_shapes=[pltpu.VMEM((tm, D), BF16)],
        compiler_params=_params(1),
        name="mix",
    )(hm, proj, proj, proj, proj, x2, conv_w, w_bm, w_bc, w_out, g2)


HALO = 8


def _ffn_kernel(hn_ref, h1_ref, wu_ref, cw_ref, cb_ref, wd_ref, g3_ref, out_ref,
                u_ref, a_ref, carry_ref, *, nj, blocks_per_seq, norm_chunk=64, sub=512):
    s = pl.program_id(0)
    tm, tn = hn_ref.shape[0], wd_ref.shape[0]
    tile_b = jnp.maximum(s - 1, 0)
    ib, jb = tile_b // nj, tile_b % nj
    jc = (s + nj - 2) % nj

    @pl.when(s == 0)
    def _():
        u_ref[...] = jnp.zeros_like(u_ref)
        a_ref[...] = jnp.zeros_like(a_ref)
        carry_ref[...] = jnp.zeros_like(carry_ref)

    @pl.when((jc == 0) | (s == 0))
    def _():
        out_ref[...] = jnp.zeros_like(out_ref)

    prevs = [jnp.where(ib % blocks_per_seq == 0, 0.0, carry_ref[jb])]
    prevs += [u_ref[h * sub - HALO:h * sub, :] for h in range(1, tm // sub)]
    carry_ref[jb] = u_ref[tm - HALO:tm, :]

    for h in range(tm // sub):
        rows = slice(h * sub, (h + 1) * sub)
        out_ref[rows, :] += jnp.dot(a_ref[rows, :], wd_ref[...], preferred_element_type=F32)
        prev = prevs[h]
        y = _causal_conv3(u_ref[rows, :], prev[HALO - 2:HALO - 1, :], prev[HALO - 1:HALO, :],
                          cw_ref) + cb_ref[...]
        gate, val = y[:, :tn], y[:, tn:]
        a_ref[rows, :] = (gate * _sigmoid(gate) * val).astype(BF16)
        u_ref[rows, :] = jnp.dot(hn_ref[rows, :], wu_ref[...], preferred_element_type=F32)

    @pl.when((jc == nj - 1) & (s >= 2))
    def _():
        def body(r, carry):
            rows = pl.ds(pl.multiple_of(r * norm_chunk, norm_chunk), norm_chunk)
            h2 = out_ref[rows, :] + h1_ref[rows, :]
            rs = lax.rsqrt(jnp.mean(h2 * h2, axis=-1, keepdims=True) + EPS)
            out_ref[rows, :] = h2 * rs * g3_ref[...]
            return carry
        lax.fori_loop(0, tm // norm_chunk, body, 0)


def _ffn(hn2, h1, w_up_t, cw_t, cb_t, w_down_p, g3, *, seq, tm=512):
    M, D = h1.shape
    nj, _, tn2 = w_up_t.shape
    tn = tn2 // 2
    assert M % tm == 0 and seq % tm == 0 and w_down_p.shape[0] == nj * tn
    ni = M // tm
    n_tiles = ni * nj
    row_a = lambda s: (jnp.minimum(s // nj, ni - 1), 0)
    row_c = lambda s: (jnp.maximum(s - 2, 0) // nj, 0)
    tile_a = lambda s: (s % nj, 0, 0)
    tile_b = lambda s: (jnp.maximum(s - 1, 0) % nj, 0, 0)
    return pl.pallas_call(
        functools.partial(_ffn_kernel, nj=nj, blocks_per_seq=seq // tm),
        grid=(n_tiles + 2,),
        in_specs=[
            pl.BlockSpec((tm, D), row_a),
            pl.BlockSpec((tm, D), row_c),
            pl.BlockSpec((None, D, tn2), tile_a),
            pl.BlockSpec((None, 3, tn2), tile_b),
            pl.BlockSpec((None, 1, tn2), tile_b),
            pl.BlockSpec((tn, D), lambda s: ((s + nj - 2) % nj, 0)),
            pl.BlockSpec((1, D), lambda s: (0, 0)),
        ],
        out_specs=pl.BlockSpec((tm, D), row_c),
        out_shape=jax.ShapeDtypeStruct((M, D), F32),
        scratch_shapes=[
            pltpu.VMEM((tm, tn2), F32),
            pltpu.VMEM((tm, tn), BF16),
            pltpu.VMEM((nj, HALO, tn2), F32),
        ],
        compiler_params=_params(1),
        name="ffn",
    )(hn2, h1, w_up_t, cw_t, cb_t, w_down_p, g3)


def _win_prep_kernel(w_ref, tail_ref, o_ref, *, first_shifted):
    i = pl.program_id(0)
    m = w_ref[...]
    shifted = jnp.concatenate([m[GATE_COLS:, :], tail_ref[...]], axis=0)
    o_ref[...] = jnp.where(i >= first_shifted, shifted, m).astype(BF16)


def _win_prep(w_in_t, *, rows=512):
    n, D = w_in_t.shape
    n_main = n - GATE_COLS
    assert n_main % rows == 0 and MAIN_SPLIT % rows == 0 and rows % GATE_COLS == 0
    return pl.pallas_call(
        functools.partial(_win_prep_kernel, first_shifted=MAIN_SPLIT // rows),
        grid=(n_main // rows,),
        in_specs=[
            pl.BlockSpec((rows, D), lambda i: (i, 0)),
            pl.BlockSpec((GATE_COLS, D), lambda i: ((i + 1) * (rows // GATE_COLS), 0)),
        ],
        out_specs=pl.BlockSpec((rows, D), lambda i: (i, 0)),
        out_shape=jax.ShapeDtypeStruct((n_main, D), BF16),
        compiler_params=_params(1),
        name="win_prep",
    )(w_in_t, w_in_t)


def _side_casts(w_up, w_down, w_out, w_branch_m, w_branch_c, *, d_ff, tn, chunks=64, down_rows=LANES):
    D = w_up.shape[0]
    nj = pl.cdiv(d_ff, tn)
    ffp = nj * tn
    assert d_ff % LANES == 0 and tn % LANES == 0 and ffp % down_rows == 0

    def rows_cast(w):
        r = w.shape[0] // chunks
        assert r * chunks == w.shape[0] and r % 16 == 0
        return (w, _cast_rows, (r, w.shape[1]), (r, w.shape[1]), w.shape, chunks, chunks)

    up_rows = D // chunks
    assert up_rows * chunks == D and up_rows % 16 == 0
    return [
        (w_up, _cast_tile_pairs(d_ff, tn), (up_rows, 2 * d_ff), (nj, up_rows, 2 * tn),
         (nj, D, 2 * tn), chunks, chunks),
        (w_down, _cast_rows_zero_tail(d_ff), (down_rows, D), (down_rows, D), (ffp, D),
         pl.cdiv(d_ff, down_rows), ffp // down_rows),
        rows_cast(w_out), rows_cast(w_branch_m), rows_cast(w_branch_c),
    ]


def _tile_pairs(a, d_ff, tn):
    nj = pl.cdiv(d_ff, tn)
    pad = lambda h: jnp.pad(h, ((0, 0), (0, nj * tn - d_ff))).reshape(a.shape[0], nj, tn)
    both = jnp.concatenate([pad(a[:, :d_ff]), pad(a[:, d_ff:])], axis=2)
    return jnp.transpose(both, (1, 0, 2))


def kernel(x, norm_mix_g, w_in, b_in, mlstm_head_g, w_branch_m, conv_w, w_branch_c, w_out,
           norm_ffn_g, w_up, ffn_conv_w, ffn_conv_b, w_down, norm_out_g):
    B, T, D = x.shape
    M = B * T
    d_ff = w_down.shape[0]
    ff_tile = 512
    ffp = pl.cdiv(d_ff, ff_tile) * ff_tile
    x2 = x.reshape(M, D)

    g0, g1 = MAIN_SPLIT, MAIN_SPLIT + GATE_COLS
    gi = g0 + M_HEADS
    w_in_t = w_in.T
    w_main_t = _win_prep(w_in_t)
    lead = lambda a: jnp.pad(a, ((0, LANES - M_HEADS), (0, 0)))
    w_gate_t = jnp.concatenate([lead(w_in_t[g0:gi]), lead(w_in_t[gi:g1])], axis=0).astype(BF16)
    b_main = jnp.concatenate([b_in[:g0], b_in[g1:]])[None, :]
    b_gate = jnp.concatenate([lead(b_in[g0:gi, None]), lead(b_in[gi:g1, None])], axis=0).T
    cw_t = _tile_pairs(ffn_conv_w, d_ff, ff_tile)
    cb_t = _tile_pairs(ffn_conv_b[None, :], d_ff, ff_tile)

    side = _side_casts(w_up, w_down, w_out, w_branch_m, w_branch_c, d_ff=d_ff, tn=ff_tile)
    proj, gates, (w_up_t, w_down_p, w_out_b, w_bm_b, w_bc_b) = _inproj(
        x2, norm_mix_g[None, :], w_main_t, b_main, w_gate_t, b_gate, side)
    hm = _mlstm(proj, gates, mlstm_head_g[None, :], batch=B, seq=T)
    h1, hn2 = _mix(hm, proj, x2, conv_w, w_bm_b, w_bc_b, w_out_b, norm_ffn_g[None, :], seq=T)
    out = _ffn(hn2, h1, w_up_t, cw_t, cb_t, w_down_p, norm_out_g[None, :], seq=T)
    return out.reshape(B, T, D)
```

```python
import functools
import math

import jax
import jax.numpy as jnp
from jax import lax
from jax.experimental import pallas as pl
from jax.experimental.pallas import tpu as pltpu

F32 = jnp.float32
BF16 = jnp.bfloat16

EPS = 1e-6
M_HEADS = 4
M_DV = 256
M_DQK = 128
M_WIDTH = M_HEADS * M_DV
M_QK_WIDTH = M_HEADS * M_DQK
C_WIDTH = 1024
LOG_QK_SCALE = -0.5 * math.log(M_DQK)

LANES = 128
GATE_COLS = 2 * M_HEADS
MAIN_SPLIT = 2 * M_QK_WIDTH + 2 * M_WIDTH

VMEM_LIMIT = 56 * 1024 * 1024


def _params(n_axes, vmem=VMEM_LIMIT, flags=None):
    return pltpu.CompilerParams(dimension_semantics=("arbitrary",) * n_axes,
                                vmem_limit_bytes=vmem, flags=flags)


def _sigmoid(v):
    return 1.0 / (1.0 + jnp.exp(-v))


def _inproj_kernel(x_ref, g_ref, w_ref, b_ref, wg_ref, bg_ref, *rest, row_chunk, side):
    n_side = len(side)
    side_in = rest[:n_side]
    proj_ref, gates_ref = rest[n_side:n_side + 2]
    side_out = rest[n_side + 2:2 * n_side + 2]
    hn_ref = rest[-1]
    j = pl.program_id(1)
    tm = x_ref.shape[0]
    step = pl.program_id(0) * pl.num_programs(1) + j
    for (relayout, n_chunks), i_ref, o_ref in zip(side, side_in, side_out):
        relayout(i_ref, o_ref, jnp.minimum(step, n_chunks - 1))

    @pl.when(j == 0)
    def _():
        def body(r, carry):
            rows = pl.ds(pl.multiple_of(r * row_chunk, row_chunk), row_chunk)
            xb = x_ref[rows, :]
            ms = jnp.mean(xb * xb, axis=-1, keepdims=True)
            hn_ref[rows, :] = (xb * lax.rsqrt(ms + EPS) * g_ref[...]).astype(BF16)
            return carry
        lax.fori_loop(0, tm // row_chunk, body, 0)
        gates_ref[...] = _dot_nt(hn_ref[...], wg_ref[...]) + bg_ref[...]

    proj_ref[...] = (_dot_nt(hn_ref[...], w_ref[...]) + b_ref[...]).astype(proj_ref.dtype)


def _dot_nt(a, b_t):
    return lax.dot_general(a, b_t, (((1,), (1,)), ((), ())), preferred_element_type=F32)


def _cast_rows(i_ref, o_ref, chunk):
    o_ref[...] = i_ref[...].astype(BF16)


def _cast_rows_zero_tail(valid_rows):
    def relayout(i_ref, o_ref, chunk):
        row = chunk * i_ref.shape[0] + lax.broadcasted_iota(jnp.int32, i_ref.shape, 0)
        o_ref[...] = jnp.where(row < valid_rows, i_ref[...], 0.0).astype(BF16)
    return relayout


def _cast_tile_pairs(d_ff, tn):
    def relayout(i_ref, o_ref, chunk):
        for t in range(o_ref.shape[0]):
            lo = t * tn
            width = min(tn, d_ff - lo)
            o_ref[t, :, 0:width] = i_ref[:, lo:lo + width].astype(BF16)
            o_ref[t, :, tn:tn + width] = i_ref[:, d_ff + lo:d_ff + lo + width].astype(BF16)
            if width < tn:
                pad = jnp.zeros((o_ref.shape[1], tn - width), BF16)
                o_ref[t, :, width:tn] = pad
                o_ref[t, :, tn + width:2 * tn] = pad
    return relayout


def _inproj(x2, g, w_main_t, b_main, w_gate_t, b_gate, side, *, tm=1024, tn=1024):
    M, D = x2.shape
    N = w_main_t.shape[0]
    assert M % tm == 0 and N % tn == 0 and w_gate_t.shape == (2 * LANES, D)
    nj = N // tn
    n_steps = (M // tm) * nj

    def chunk_map(n_chunks, axis, rank):
        def index_map(i, j):
            idx = [0] * rank
            idx[axis] = jnp.minimum(i * nj + j, n_chunks - 1)
            return tuple(idx)
        return index_map

    side_in_specs, side_out_specs, side_out_shapes, side_meta = [], [], [], []
    for arr, relayout, in_block, out_block, out_shape, n_in, n_out in side:
        assert n_out <= n_steps and n_in <= n_out
        side_in_specs.append(pl.BlockSpec(in_block, chunk_map(n_in, 0, 2)))
        side_out_specs.append(pl.BlockSpec(out_block, chunk_map(n_out, len(out_block) - 2,
                                                                len(out_block))))
        side_out_shapes.append(jax.ShapeDtypeStruct(out_shape, BF16))
        side_meta.append((relayout, n_out))

    outs = pl.pallas_call(
        functools.partial(_inproj_kernel, row_chunk=128, side=tuple(side_meta)),
        grid=(M // tm, nj),
        in_specs=[
            pl.BlockSpec((tm, D), lambda i, j: (i, 0)),
            pl.BlockSpec((1, D), lambda i, j: (0, 0)),
            pl.BlockSpec((tn, D), lambda i, j: (j, 0)),
            pl.BlockSpec((1, tn), lambda i, j: (0, j)),
            pl.BlockSpec((2 * LANES, D), lambda i, j: (0, 0)),
            pl.BlockSpec((1, 2 * LANES), lambda i, j: (0, 0)),
        ] + side_in_specs,
        out_specs=[
            pl.BlockSpec((tm, tn), lambda i, j: (i, j)),
            pl.BlockSpec((tm, 2 * LANES), lambda i, j: (i, 0)),
        ] + side_out_specs,
        out_shape=[
            jax.ShapeDtypeStruct((M, N), BF16),
            jax.ShapeDtypeStruct((M, 2 * LANES), F32),
        ] + side_out_shapes,
        scratch_shapes=[pltpu.VMEM((tm, D), BF16)],
        compiler_params=_params(2),
        name="inproj",
    )(x2, g, w_main_t, b_main, w_gate_t, b_gate, *[s[0] for s in side])
    return outs[0], outs[1], outs[2:]


def _split3(a):
    hi = a.astype(BF16)
    r1 = a - hi.astype(F32)
    mid = r1.astype(BF16)
    lo = (r1 - mid.astype(F32)).astype(BF16)
    return hi, mid, lo


def _mlstm_kernel(p_ref, g_ref, hg_ref, out_ref, st_ref, m_ref):
    c = pl.program_id(1)
    L = p_ref.shape[0]

    @pl.when(c == 0)
    def _():
        st_ref[...] = jnp.zeros_like(st_ref)
        m_ref[...] = jnp.zeros_like(m_ref)

    logi = g_ref[:, :LANES]
    f_pre = g_ref[:, LANES:]
    logf = jnp.minimum(f_pre, 0.0) - jnp.log1p(jnp.exp(-jnp.abs(f_pre)))

    row = lax.broadcasted_iota(jnp.int32, (L, L), 0)
    col = lax.broadcasted_iota(jnp.int32, (L, L), 1)
    causal = row >= col
    tri = causal.astype(BF16)
    parts = _split3(logf)
    b = (jnp.dot(tri, parts[2], preferred_element_type=F32)
         + jnp.dot(tri, parts[1], preferred_element_type=F32)
         + jnp.dot(tri, parts[0], preferred_element_type=F32))
    g = logi - b

    trow = lax.broadcasted_iota(jnp.int32, g.shape, 0)
    run = g
    shift = 1
    while shift < L:
        run = jnp.maximum(run, jnp.where(trow >= shift, pltpu.roll(run, shift, 0), -jnp.inf))
        shift *= 2
    m_prev_row = m_ref[0:1, :]
    big_m = jnp.maximum(run, m_prev_row)
    m_last = big_m[L - 1:L, :]
    m_ref[...] = jnp.broadcast_to(b[L - 1:L, :] + m_last, m_ref.shape)
    e_col = jnp.exp(-(b + big_m))
    g_t = g.T

    ones_blk = jnp.ones((L, LANES), BF16)
    for h in range(M_HEADS):
        q = p_ref[:, h * M_DQK:(h + 1) * M_DQK]
        k = p_ref[:, M_QK_WIDTH + h * M_DQK:M_QK_WIDTH + (h + 1) * M_DQK]
        v = p_ref[:, 2 * M_QK_WIDTH + h * M_DV:2 * M_QK_WIDTH + (h + 1) * M_DV]
        o_pre = p_ref[:, 2 * M_QK_WIDTH + M_WIDTH + h * M_DV:
                      2 * M_QK_WIDTH + M_WIDTH + (h + 1) * M_DV].astype(F32)
        v_ext = jnp.concatenate([v, ones_blk], axis=1)

        g_row = g_t[h:h + 1, :]
        m_rep = jnp.broadcast_to(big_m[:, h:h + 1], (L, LANES))
        m_prev = m_prev_row[:, h:h + 1]
        m_end = m_last[:, h:h + 1]

        arg = (g_row + LOG_QK_SCALE) - jnp.concatenate([m_rep] * (L // LANES), axis=1)
        qk = lax.dot_general(q, k, (((1,), (1,)), ((), ())), preferred_element_type=F32)
        S = (qk * jnp.exp(jnp.where(causal, arg, -jnp.inf))).astype(BF16)
        intra = jnp.dot(S, v_ext, preferred_element_type=F32)
        Ct = st_ref[h]
        inter = jnp.dot(q, Ct.astype(BF16), preferred_element_type=F32)
        isc = jnp.exp((m_prev + LOG_QK_SCALE) - m_rep)
        tot = intra + jnp.concatenate([isc] * (intra.shape[1] // LANES), axis=1) * inter
        num = tot[:, :M_DV]

        den = tot[:, M_DV:M_DV + 1]
        f = 1.0 / jnp.maximum(jnp.abs(den), e_col[:, h:h + 1])
        ms = jnp.mean(num * num, axis=-1, keepdims=True)
        fac = f * lax.rsqrt(ms * (f * f) + EPS)
        hh = num * fac * hg_ref[:, h * M_DV:(h + 1) * M_DV]
        out_ref[:, h * M_DV:(h + 1) * M_DV] = (hh * _sigmoid(o_pre)).astype(out_ref.dtype)

        kw_t = (k.T.astype(F32) * jnp.exp(g_row - m_end)).astype(BF16)
        upd = jnp.dot(kw_t, v_ext, preferred_element_type=F32)
        st_ref[h] = jnp.exp(m_prev - m_end) * Ct + upd


def _mlstm(proj, gates, head_g, *, batch, seq, chunk=256):
    M = proj.shape[0]
    nc = seq // chunk
    assert seq % chunk == 0 and M == batch * seq and chunk % LANES == 0
    width = 2 * M_QK_WIDTH + 2 * M_WIDTH
    return pl.pallas_call(
        _mlstm_kernel,
        grid=(batch, nc),
        in_specs=[
            pl.BlockSpec((chunk, width), lambda b, c: (b * nc + c, 0)),
            pl.BlockSpec((chunk, 2 * LANES), lambda b, c: (b * nc + c, 0)),
            pl.BlockSpec((1, M_WIDTH), lambda b, c: (0, 0)),
        ],
        out_specs=pl.BlockSpec((chunk, M_WIDTH), lambda b, c: (b * nc + c, 0)),
        out_shape=jax.ShapeDtypeStruct((M, M_WIDTH), BF16),
        scratch_shapes=[
            pltpu.VMEM((M_HEADS, M_DQK, M_DV + LANES), F32),
            pltpu.VMEM((8, LANES), F32),
        ],
        compiler_params=_params(2),
        name="mlstm",
    )(proj, gates, head_g)


HALO = 8


def _causal_conv3(cur, prev, w_ref):
    n = prev.shape[0]
    ext = jnp.concatenate([prev, cur], axis=0)
    r1 = pltpu.roll(ext, 1, 0)[n:, :]
    r2 = pltpu.roll(ext, 2, 0)[n:, :]
    return r2 * w_ref[0:1, :] + r1 * w_ref[1:2, :] + cur * w_ref[2:3, :]


def _mix_kernel(hm_ref, cb_ref, halo_ref, gm_ref, gc_ref, x_ref, cw_ref, wbm_ref, wbc_ref,
                wout_ref, g2_ref, h1_ref, hn2_ref, merged_ref, *, blocks_per_seq, col_chunk):
    i = pl.program_id(0)
    tm = hm_ref.shape[0]
    D = x_ref.shape[1]

    cb = cb_ref[:, 0:C_WIDTH].astype(F32)
    pp = cb_ref[:, C_WIDTH:2 * C_WIDTH].astype(F32) * cb_ref[:, 2 * C_WIDTH:3 * C_WIDTH].astype(F32)
    hp = halo_ref[:, 0:C_WIDTH].astype(F32) * halo_ref[:, C_WIDTH:2 * C_WIDTH].astype(F32)
    hp = jnp.where(i % blocks_per_seq != 0, hp, 0.0)
    z = (cb * _causal_conv3(pp, hp[halo_ref.shape[0] - HALO:, :], cw_ref)).astype(BF16)
    hm = hm_ref[...]

    for n in range(D // col_chunk):
        cols = slice(n * col_chunk, (n + 1) * col_chunk)
        y_m = jnp.dot(hm, wbm_ref[:, cols], preferred_element_type=F32)
        y_c = jnp.dot(z, wbc_ref[:, cols], preferred_element_type=F32)
        merged_ref[:, cols] = (_sigmoid(gm_ref[:, cols].astype(F32)) * y_m
                               + _sigmoid(gc_ref[:, cols].astype(F32)) * y_c).astype(BF16)

    ssq = jnp.zeros((tm, 1), F32)
    for n in range(D // col_chunk):
        cols = slice(n * col_chunk, (n + 1) * col_chunk)
        h1 = x_ref[:, cols] + jnp.dot(merged_ref[...], wout_ref[:, cols],
                                      preferred_element_type=F32)
        h1_ref[:, cols] = h1
        ssq = ssq + jnp.sum(h1 * h1, axis=-1, keepdims=True)
    rs = lax.rsqrt(ssq / D + EPS)
    for n in range(D // col_chunk):
        cols = slice(n * col_chunk, (n + 1) * col_chunk)
        hn2_ref[:, cols] = (h1_ref[:, cols] * rs * g2_ref[:, cols]).astype(hn2_ref.dtype)


def _mix(hm, proj, x2, conv_w, w_bm, w_bc, w_out, g2, *, seq, tm=256, halo=16):
    M, D = x2.shape
    assert M % tm == 0 and seq % tm == 0 and tm % halo == 0
    cb_blk = MAIN_SPLIT // (3 * C_WIDTH)
    assert cb_blk * 3 * C_WIDTH == MAIN_SPLIT
    cc_off = MAIN_SPLIT + C_WIDTH
    gm_off = MAIN_SPLIT + 3 * C_WIDTH
    assert cc_off % (2 * C_WIDTH) == 0 and gm_off % D == 0
    const = dict(pipeline_mode=pl.Buffered(1))
    return pl.pallas_call(
        functools.partial(_mix_kernel, blocks_per_seq=seq // tm, col_chunk=512),
        grid=(M // tm,),
        in_specs=[
            pl.BlockSpec((tm, M_WIDTH), lambda i: (i, 0)),
            pl.BlockSpec((tm, 3 * C_WIDTH), lambda i: (i, cb_blk)),
            pl.BlockSpec((halo, 2 * C_WIDTH),
                         lambda i: (jnp.maximum(i * (tm // halo) - 1, 0), cc_off // (2 * C_WIDTH))),
            pl.BlockSpec((tm, D), lambda i: (i, gm_off // D)),
            pl.BlockSpec((tm, D), lambda i: (i, gm_off // D + 1)),
            pl.BlockSpec((tm, D), lambda i: (i, 0)),
            pl.BlockSpec(conv_w.shape, lambda i: (0, 0)),
            pl.BlockSpec(w_bm.shape, lambda i: (0, 0), **const),
            pl.BlockSpec(w_bc.shape, lambda i: (0, 0), **const),
            pl.BlockSpec(w_out.shape, lambda i: (0, 0), **const),
            pl.BlockSpec((1, D), lambda i: (0, 0)),
        ],
        out_specs=[
            pl.BlockSpec((tm, D), lambda i: (i, 0)),
            pl.BlockSpec((tm, D), lambda i: (i, 0)),
        ],
        out_shape=[
            jax.ShapeDtypeStruct((M, D), F32),
            jax.ShapeDtypeStruct((M, D), BF16),
        ],
        scratch_shapes=[pltpu.VMEM((tm, D), BF16)],
        compiler_params=_params(1),
        name="mix",
    )(hm, proj, proj, proj, proj, x2, conv_w, w_bm, w_bc, w_out, g2)


def _ffn_kernel(hn_ref, h1_ref, wu_ref, cw_ref, cb_ref, wd_ref, g3_ref, out_ref,
                u_ref, a_ref, carry_ref, *, nj, blocks_per_seq, norm_chunk=64, sub=512):
    s = pl.program_id(0)
    tm, tn = hn_ref.shape[0], wd_ref.shape[0]
    tile_b = jnp.maximum(s - 1, 0)
    ib, jb = tile_b // nj, tile_b % nj
    jc = (s + nj - 2) % nj

    @pl.when(s == 0)
    def _():
        u_ref[...] = jnp.zeros_like(u_ref)
        a_ref[...] = jnp.zeros_like(a_ref)
        carry_ref[...] = jnp.zeros_like(carry_ref)

    @pl.when((jc == 0) | (s == 0))
    def _():
        out_ref[...] = jnp.zeros_like(out_ref)

    prevs = [jnp.where(ib % blocks_per_seq == 0, 0.0, carry_ref[jb])]
    prevs += [u_ref[h * sub - HALO:h * sub, :] for h in range(1, tm // sub)]
    carry_ref[jb] = u_ref[tm - HALO:tm, :]

    for h in range(tm // sub):
        rows = slice(h * sub, (h + 1) * sub)
        out_ref[rows, :] += jnp.dot(a_ref[rows, :], wd_ref[...], preferred_element_type=F32)
        y = _causal_conv3(u_ref[rows, :], prevs[h], cw_ref) + cb_ref[...]
        gate, val = y[:, :tn], y[:, tn:]
        a_ref[rows, :] = (gate * _sigmoid(gate) * val).astype(BF16)
        u_ref[rows, :] = jnp.dot(hn_ref[rows, :], wu_ref[...], preferred_element_type=F32)

    @pl.when((jc == nj - 1) & (s >= 2))
    def _():
        def body(r, carry):
            rows = pl.ds(pl.multiple_of(r * norm_chunk, norm_chunk), norm_chunk)
            h2 = out_ref[rows, :] + h1_ref[rows, :]
            rs = lax.rsqrt(jnp.mean(h2 * h2, axis=-1, keepdims=True) + EPS)
            out_ref[rows, :] = h2 * rs * g3_ref[...]
            return carry
        lax.fori_loop(0, tm // norm_chunk, body, 0)


def _ffn(hn2, h1, w_up_t, cw_t, cb_t, w_down_p, g3, *, seq, tm=512):
    M, D = h1.shape
    nj, _, tn2 = w_up_t.shape
    tn = tn2 // 2
    assert M % tm == 0 and seq % tm == 0 and w_down_p.shape[0] == nj * tn
    ni = M // tm
    n_tiles = ni * nj
    row_a = lambda s: (jnp.minimum(s // nj, ni - 1), 0)
    row_c = lambda s: (jnp.maximum(s - 2, 0) // nj, 0)
    tile_a = lambda s: (s % nj, 0, 0)
    tile_b = lambda s: (jnp.maximum(s - 1, 0) % nj, 0, 0)
    return pl.pallas_call(
        functools.partial(_ffn_kernel, nj=nj, blocks_per_seq=seq // tm),
        grid=(n_tiles + 2,),
        in_specs=[
            pl.BlockSpec((tm, D), row_a),
            pl.BlockSpec((tm, D), row_c),
            pl.BlockSpec((None, D, tn2), tile_a),
            pl.BlockSpec((None, 3, tn2), tile_b),
            pl.BlockSpec((None, 1, tn2), tile_b),
            pl.BlockSpec((tn, D), lambda s: ((s + nj - 2) % nj, 0)),
            pl.BlockSpec((1, D), lambda s: (0, 0)),
        ],
        out_specs=pl.BlockSpec((tm, D), row_c),
        out_shape=jax.ShapeDtypeStruct((M, D), F32),
        scratch_shapes=[
            pltpu.VMEM((tm, tn2), F32),
            pltpu.VMEM((tm, tn), BF16),
            pltpu.VMEM((nj, HALO, tn2), F32),
        ],
        compiler_params=_params(1),
        name="ffn",
    )(hn2, h1, w_up_t, cw_t, cb_t, w_down_p, g3)


def _win_prep_kernel(w_ref, tail_ref, gate_ref, o_ref, og_ref, *, first_shifted):
    i = pl.program_id(0)
    m = w_ref[...]
    shifted = jnp.concatenate([m[GATE_COLS:, :], tail_ref[...]], axis=0)
    o_ref[...] = jnp.where(i >= first_shifted, shifted, m).astype(BF16)

    gr = gate_ref[...]
    lead = lax.broadcasted_iota(jnp.int32, gr.shape, 0) < M_HEADS
    zeros = jnp.zeros((LANES - GATE_COLS, gr.shape[1]), F32)
    og_ref[...] = jnp.concatenate(
        [jnp.where(lead, gr, 0.0), zeros,
         jnp.where(lead, pltpu.roll(gr, M_HEADS, 0), 0.0), zeros], axis=0).astype(BF16)


def _win_prep(w_in_t, *, rows=512):
    n, D = w_in_t.shape
    n_main = n - GATE_COLS
    assert n_main % rows == 0 and MAIN_SPLIT % rows == 0 and rows % GATE_COLS == 0
    return pl.pallas_call(
        functools.partial(_win_prep_kernel, first_shifted=MAIN_SPLIT // rows),
        grid=(n_main // rows,),
        in_specs=[
            pl.BlockSpec((rows, D), lambda i: (i, 0)),
            pl.BlockSpec((GATE_COLS, D), lambda i: ((i + 1) * (rows // GATE_COLS), 0)),
            pl.BlockSpec((GATE_COLS, D), lambda i: (MAIN_SPLIT // GATE_COLS, 0)),
        ],
        out_specs=[
            pl.BlockSpec((rows, D), lambda i: (i, 0)),
            pl.BlockSpec((2 * LANES, D), lambda i: (0, 0)),
        ],
        out_shape=[
            jax.ShapeDtypeStruct((n_main, D), BF16),
            jax.ShapeDtypeStruct((2 * LANES, D), BF16),
        ],
        compiler_params=_params(1),
        name="win_prep",
    )(w_in_t, w_in_t, w_in_t)


def _side_casts(w_up, w_down, w_out, w_branch_m, w_branch_c, *, d_ff, tn, chunks=64, down_rows=LANES):
    D = w_up.shape[0]
    nj = pl.cdiv(d_ff, tn)
    ffp = nj * tn
    assert d_ff % LANES == 0 and tn % LANES == 0 and ffp % down_rows == 0

    def rows_cast(w):
        r = w.shape[0] // chunks
        assert r * chunks == w.shape[0] and r % 16 == 0
        return (w, _cast_rows, (r, w.shape[1]), (r, w.shape[1]), w.shape, chunks, chunks)

    up_rows = D // chunks
    assert up_rows * chunks == D and up_rows % 16 == 0
    return [
        (w_up, _cast_tile_pairs(d_ff, tn), (up_rows, 2 * d_ff), (nj, up_rows, 2 * tn),
         (nj, D, 2 * tn), chunks, chunks),
        (w_down, _cast_rows_zero_tail(d_ff), (down_rows, D), (down_rows, D), (ffp, D),
         pl.cdiv(d_ff, down_rows), ffp // down_rows),
        rows_cast(w_out), rows_cast(w_branch_m), rows_cast(w_branch_c),
    ]


def _tile_pairs(a, d_ff, tn):
    nj = pl.cdiv(d_ff, tn)
    pad = lambda h: jnp.pad(h, ((0, 0), (0, nj * tn - d_ff))).reshape(a.shape[0], nj, tn)
    both = jnp.concatenate([pad(a[:, :d_ff]), pad(a[:, d_ff:])], axis=2)
    return jnp.transpose(both, (1, 0, 2))


def kernel(x, norm_mix_g, w_in, b_in, mlstm_head_g, w_branch_m, conv_w, w_branch_c, w_out,
           norm_ffn_g, w_up, ffn_conv_w, ffn_conv_b, w_down, norm_out_g):
    B, T, D = x.shape
    M = B * T
    d_ff = w_down.shape[0]
    ff_tile = 512
    ffp = pl.cdiv(d_ff, ff_tile) * ff_tile
    x2 = x.reshape(M, D)

    g0, g1 = MAIN_SPLIT, MAIN_SPLIT + GATE_COLS
    gi = g0 + M_HEADS
    w_in_t = w_in.T
    w_main_t, w_gate_t = _win_prep(w_in_t)
    lead = lambda a: jnp.pad(a, ((0, LANES - M_HEADS), (0, 0)))
    b_main = jnp.concatenate([b_in[:g0], b_in[g1:]])[None, :]
    b_gate = jnp.concatenate([lead(b_in[g0:gi, None]), lead(b_in[gi:g1, None])], axis=0).T
    cw_t = _tile_pairs(ffn_conv_w, d_ff, ff_tile)
    cb_t = _tile_pairs(ffn_conv_b[None, :], d_ff, ff_tile)

    side = _side_casts(w_up, w_down, w_out, w_branch_m, w_branch_c, d_ff=d_ff, tn=ff_tile)
    proj, gates, (w_up_t, w_down_p, w_out_b, w_bm_b, w_bc_b) = _inproj(
        x2, norm_mix_g[None, :], w_main_t, b_main, w_gate_t, b_gate, side)
    hm = _mlstm(proj, gates, mlstm_head_g[None, :], batch=B, seq=T)
    h1, hn2 = _mix(hm, proj, x2, conv_w, w_bm_b, w_bc_b, w_out_b, norm_ffn_g[None, :], seq=T)
    out = _ffn(hn2, h1, w_up_t, cw_t, cb_t, w_down_p, norm_out_g[None, :], seq=T)
    return out.reshape(B, T, D)
```

```python
import functools
import math

import jax
import jax.numpy as jnp
from jax import lax
from jax.experimental import pallas as pl
from jax.experimental.pallas import tpu as pltpu

F32 = jnp.float32
BF16 = jnp.bfloat16

EPS = 1e-6
M_HEADS = 4
M_DV = 256
M_DQK = 128
M_WIDTH = M_HEADS * M_DV
M_QK_WIDTH = M_HEADS * M_DQK
C_WIDTH = 1024
LOG_QK_SCALE = -0.5 * math.log(M_DQK)

LANES = 128
GATE_COLS = 2 * M_HEADS
MAIN_SPLIT = 2 * M_QK_WIDTH + 2 * M_WIDTH

VMEM_LIMIT = 56 * 1024 * 1024


def _params(n_axes, vmem=VMEM_LIMIT, flags=None):
    return pltpu.CompilerParams(dimension_semantics=("arbitrary",) * n_axes,
                                vmem_limit_bytes=vmem, flags=flags)


def _sigmoid(v):
    return 1.0 / (1.0 + jnp.exp(-v))


def _inproj_kernel(x_ref, g_ref, w_ref, b_ref, wg_ref, bg_ref, *rest, row_chunk, side):
    n_side = len(side)
    side_in = rest[:n_side]
    proj_ref, gates_ref = rest[n_side:n_side + 2]
    side_out = rest[n_side + 2:2 * n_side + 2]
    hn_ref = rest[-1]
    j = pl.program_id(1)
    tm = x_ref.shape[0]
    step = pl.program_id(0) * pl.num_programs(1) + j
    for (relayout, n_chunks), i_ref, o_ref in zip(side, side_in, side_out):
        relayout(i_ref, o_ref, jnp.minimum(step, n_chunks - 1))

    @pl.when(j == 0)
    def _():
        def body(r, carry):
            rows = pl.ds(pl.multiple_of(r * row_chunk, row_chunk), row_chunk)
            xb = x_ref[rows, :]
            ms = jnp.mean(xb * xb, axis=-1, keepdims=True)
            hn_ref[rows, :] = (xb * lax.rsqrt(ms + EPS) * g_ref[...]).astype(BF16)
            return carry
        lax.fori_loop(0, tm // row_chunk, body, 0, unroll=True)
        gates_ref[...] = _dot_nt(hn_ref[...], wg_ref[...]) + bg_ref[...]

    proj_ref[...] = (_dot_nt(hn_ref[...], w_ref[...]) + b_ref[...]).astype(proj_ref.dtype)


def _dot_nt(a, b_t):
    return lax.dot_general(a, b_t, (((1,), (1,)), ((), ())), preferred_element_type=F32)


def _cast_rows(i_ref, o_ref, chunk):
    o_ref[...] = i_ref[...].astype(BF16)


def _cast_rows_zero_tail(valid_rows):
    def relayout(i_ref, o_ref, chunk):
        row = chunk * i_ref.shape[0] + lax.broadcasted_iota(jnp.int32, i_ref.shape, 0)
        o_ref[...] = jnp.where(row < valid_rows, i_ref[...], 0.0).astype(BF16)
    return relayout


def _cast_tile_pairs(d_ff, tn):
    def relayout(i_ref, o_ref, chunk):
        for t in range(o_ref.shape[0]):
            lo = t * tn
            width = min(tn, d_ff - lo)
            o_ref[t, :, 0:width] = i_ref[:, lo:lo + width].astype(BF16)
            o_ref[t, :, tn:tn + width] = i_ref[:, d_ff + lo:d_ff + lo + width].astype(BF16)
            if width < tn:
                pad = jnp.zeros((o_ref.shape[1], tn - width), BF16)
                o_ref[t, :, width:tn] = pad
                o_ref[t, :, tn + width:2 * tn] = pad
    return relayout


def _inproj(x2, g, w_main_t, b_main, w_gate_t, b_gate, side, *, tm=1024, tn=1024):
    M, D = x2.shape
    N = w_main_t.shape[0]
    assert M % tm == 0 and N % tn == 0 and w_gate_t.shape == (2 * LANES, D)
    nj = N // tn
    n_steps = (M // tm) * nj

    def chunk_map(n_chunks, axis, rank):
        def index_map(i, j):
            idx = [0] * rank
            idx[axis] = jnp.minimum(i * nj + j, n_chunks - 1)
            return tuple(idx)
        return index_map

    side_in_specs, side_out_specs, side_out_shapes, side_meta = [], [], [], []
    for arr, relayout, in_block, out_block, out_shape, n_in, n_out in side:
        assert n_out <= n_steps and n_in <= n_out
        side_in_specs.append(pl.BlockSpec(in_block, chunk_map(n_in, 0, 2)))
        side_out_specs.append(pl.BlockSpec(out_block, chunk_map(n_out, len(out_block) - 2,
                                                                len(out_block))))
        side_out_shapes.append(jax.ShapeDtypeStruct(out_shape, BF16))
        side_meta.append((relayout, n_out))

    outs = pl.pallas_call(
        functools.partial(_inproj_kernel, row_chunk=128, side=tuple(side_meta)),
        grid=(M // tm, nj),
        in_specs=[
            pl.BlockSpec((tm, D), lambda i, j: (i, 0)),
            pl.BlockSpec((1, D), lambda i, j: (0, 0)),
            pl.BlockSpec((tn, D), lambda i, j: (j, 0)),
            pl.BlockSpec((1, tn), lambda i, j: (0, j)),
            pl.BlockSpec((2 * LANES, D), lambda i, j: (0, 0)),
            pl.BlockSpec((1, 2 * LANES), lambda i, j: (0, 0)),
        ] + side_in_specs,
        out_specs=[
            pl.BlockSpec((tm, tn), lambda i, j: (i, j)),
            pl.BlockSpec((tm, 2 * LANES), lambda i, j: (i, 0)),
        ] + side_out_specs,
        out_shape=[
            jax.ShapeDtypeStruct((M, N), BF16),
            jax.ShapeDtypeStruct((M, 2 * LANES), F32),
        ] + side_out_shapes,
        scratch_shapes=[pltpu.VMEM((tm, D), BF16)],
        compiler_params=_params(2),
        name="inproj",
    )(x2, g, w_main_t, b_main, w_gate_t, b_gate, *[s[0] for s in side])
    return outs[0], outs[1], outs[2:]


def _split3(a):
    hi = a.astype(BF16)
    r1 = a - hi.astype(F32)
    mid = r1.astype(BF16)
    lo = (r1 - mid.astype(F32)).astype(BF16)
    return hi, mid, lo


def _mlstm_kernel(p_ref, g_ref, hg_ref, out_ref, st_ref, m_ref):
    c = pl.program_id(1)
    L = p_ref.shape[0]

    @pl.when(c == 0)
    def _():
        st_ref[...] = jnp.zeros_like(st_ref)
        m_ref[...] = jnp.zeros_like(m_ref)

    logi = g_ref[:, :LANES]
    f_pre = g_ref[:, LANES:]
    logf = jnp.minimum(f_pre, 0.0) - jnp.log1p(jnp.exp(-jnp.abs(f_pre)))

    row = lax.broadcasted_iota(jnp.int32, (L, L), 0)
    col = lax.broadcasted_iota(jnp.int32, (L, L), 1)
    causal = row >= col
    tri = causal.astype(BF16)
    parts = _split3(logf)
    b = (jnp.dot(tri, parts[2], preferred_element_type=F32)
         + jnp.dot(tri, parts[1], preferred_element_type=F32)
         + jnp.dot(tri, parts[0], preferred_element_type=F32))
    g = logi - b

    trow = lax.broadcasted_iota(jnp.int32, g.shape, 0)
    run = g
    shift = 1
    while shift < L:
        run = jnp.maximum(run, jnp.where(trow >= shift, pltpu.roll(run, shift, 0), -jnp.inf))
        shift *= 2
    m_prev_row = m_ref[0:1, :]
    big_m = jnp.maximum(run, m_prev_row)
    m_last = big_m[L - 1:L, :]
    m_ref[...] = jnp.broadcast_to(b[L - 1:L, :] + m_last, m_ref.shape)
    e_col = jnp.exp(-(b + big_m))
    g_t = g.T

    ones_blk = jnp.ones((L, LANES), BF16)
    for h in range(M_HEADS):
        q = p_ref[:, h * M_DQK:(h + 1) * M_DQK]
        k = p_ref[:, M_QK_WIDTH + h * M_DQK:M_QK_WIDTH + (h + 1) * M_DQK]
        v = p_ref[:, 2 * M_QK_WIDTH + h * M_DV:2 * M_QK_WIDTH + (h + 1) * M_DV]
        o_pre = p_ref[:, 2 * M_QK_WIDTH + M_WIDTH + h * M_DV:
                      2 * M_QK_WIDTH + M_WIDTH + (h + 1) * M_DV].astype(F32)
        v_ext = jnp.concatenate([v, ones_blk], axis=1)

        g_row = g_t[h:h + 1, :]
        m_rep = jnp.broadcast_to(big_m[:, h:h + 1], (L, LANES))
        m_prev = m_prev_row[:, h:h + 1]
        m_end = m_last[:, h:h + 1]

        arg = (g_row + LOG_QK_SCALE) - jnp.concatenate([m_rep] * (L // LANES), axis=1)
        qk = lax.dot_general(q, k, (((1,), (1,)), ((), ())), preferred_element_type=F32)
        S = (qk * jnp.exp(jnp.where(causal, arg, -jnp.inf))).astype(BF16)
        intra = jnp.dot(S, v_ext, preferred_element_type=F32)
        Ct = st_ref[h]
        inter = jnp.dot(q, Ct.astype(BF16), preferred_element_type=F32)
        isc = jnp.exp((m_prev + LOG_QK_SCALE) - m_rep)
        tot = intra + jnp.concatenate([isc] * (intra.shape[1] // LANES), axis=1) * inter
        num = tot[:, :M_DV]

        den = tot[:, M_DV:M_DV + 1]
        f = 1.0 / jnp.maximum(jnp.abs(den), e_col[:, h:h + 1])
        ms = jnp.mean(num * num, axis=-1, keepdims=True)
        fac = f * lax.rsqrt(ms * (f * f) + EPS)
        hh = num * fac * hg_ref[:, h * M_DV:(h + 1) * M_DV]
        out_ref[:, h * M_DV:(h + 1) * M_DV] = (hh * _sigmoid(o_pre)).astype(out_ref.dtype)

        kw_t = (k.T.astype(F32) * jnp.exp(g_row - m_end)).astype(BF16)
        upd = jnp.dot(kw_t, v_ext, preferred_element_type=F32)
        st_ref[h] = jnp.exp(m_prev - m_end) * Ct + upd


def _mlstm(proj, gates, head_g, *, batch, seq, chunk=256):
    M = proj.shape[0]
    nc = seq // chunk
    assert seq % chunk == 0 and M == batch * seq and chunk % LANES == 0
    width = 2 * M_QK_WIDTH + 2 * M_WIDTH
    return pl.pallas_call(
        _mlstm_kernel,
        grid=(batch, nc),
        in_specs=[
            pl.BlockSpec((chunk, width), lambda b, c: (b * nc + c, 0)),
            pl.BlockSpec((chunk, 2 * LANES), lambda b, c: (b * nc + c, 0)),
            pl.BlockSpec((1, M_WIDTH), lambda b, c: (0, 0)),
        ],
        out_specs=pl.BlockSpec((chunk, M_WIDTH), lambda b, c: (b * nc + c, 0)),
        out_shape=jax.ShapeDtypeStruct((M, M_WIDTH), BF16),
        scratch_shapes=[
            pltpu.VMEM((M_HEADS, M_DQK, M_DV + LANES), F32),
            pltpu.VMEM((8, LANES), F32),
        ],
        compiler_params=_params(2),
        name="mlstm",
    )(proj, gates, head_g)


HALO = 8


def _causal_conv3(cur, prev, w_ref):
    n = prev.shape[0]
    ext = jnp.concatenate([prev, cur], axis=0)
    r1 = pltpu.roll(ext, 1, 0)[n:, :]
    r2 = pltpu.roll(ext, 2, 0)[n:, :]
    return r2 * w_ref[0:1, :] + r1 * w_ref[1:2, :] + cur * w_ref[2:3, :]


def _mix_kernel(hm_ref, cb_ref, halo_ref, gm_ref, gc_ref, x_ref, cw_ref, wbm_ref, wbc_ref,
                wout_ref, g2_ref, h1_ref, hn2_ref, merged_ref, *, blocks_per_seq, col_chunk):
    i = pl.program_id(0)
    tm = hm_ref.shape[0]
    D = x_ref.shape[1]

    cb = cb_ref[:, 0:C_WIDTH].astype(F32)
    pp = cb_ref[:, C_WIDTH:2 * C_WIDTH].astype(F32) * cb_ref[:, 2 * C_WIDTH:3 * C_WIDTH].astype(F32)
    hp = halo_ref[:, 0:C_WIDTH].astype(F32) * halo_ref[:, C_WIDTH:2 * C_WIDTH].astype(F32)
    hp = jnp.where(i % blocks_per_seq != 0, hp, 0.0)
    z = (cb * _causal_conv3(pp, hp[halo_ref.shape[0] - HALO:, :], cw_ref)).astype(BF16)
    hm = hm_ref[...]

    for n in range(D // col_chunk):
        cols = slice(n * col_chunk, (n + 1) * col_chunk)
        y_m = jnp.dot(hm, wbm_ref[:, cols], preferred_element_type=F32)
        y_c = jnp.dot(z, wbc_ref[:, cols], preferred_element_type=F32)
        merged_ref[:, cols] = (_sigmoid(gm_ref[:, cols].astype(F32)) * y_m
                               + _sigmoid(gc_ref[:, cols].astype(F32)) * y_c).astype(BF16)

    ssq = jnp.zeros((tm, 1), F32)
    for n in range(D // col_chunk):
        cols = slice(n * col_chunk, (n + 1) * col_chunk)
        h1 = x_ref[:, cols] + jnp.dot(merged_ref[...], wout_ref[:, cols],
                                      preferred_element_type=F32)
        h1_ref[:, cols] = h1
        ssq = ssq + jnp.sum(h1 * h1, axis=-1, keepdims=True)
    rs = lax.rsqrt(ssq / D + EPS)
    for n in range(D // col_chunk):
        cols = slice(n * col_chunk, (n + 1) * col_chunk)
        hn2_ref[:, cols] = (h1_ref[:, cols] * rs * g2_ref[:, cols]).astype(hn2_ref.dtype)


def _mix(hm, proj, x2, conv_w, w_bm, w_bc, w_out, g2, *, seq, tm=256, halo=16):
    M, D = x2.shape
    assert M % tm == 0 and seq % tm == 0 and tm % halo == 0
    cb_blk = MAIN_SPLIT // (3 * C_WIDTH)
    assert cb_blk * 3 * C_WIDTH == MAIN_SPLIT
    cc_off = MAIN_SPLIT + C_WIDTH
    gm_off = MAIN_SPLIT + 3 * C_WIDTH
    assert cc_off % (2 * C_WIDTH) == 0 and gm_off % D == 0
    const = dict(pipeline_mode=pl.Buffered(1))
    return pl.pallas_call(
        functools.partial(_mix_kernel, blocks_per_seq=seq // tm, col_chunk=512),
        grid=(M // tm,),
        in_specs=[
            pl.BlockSpec((tm, M_WIDTH), lambda i: (i, 0)),
            pl.BlockSpec((tm, 3 * C_WIDTH), lambda i: (i, cb_blk)),
            pl.BlockSpec((halo, 2 * C_WIDTH),
                         lambda i: (jnp.maximum(i * (tm // halo) - 1, 0), cc_off // (2 * C_WIDTH))),
            pl.BlockSpec((tm, D), lambda i: (i, gm_off // D)),
            pl.BlockSpec((tm, D), lambda i: (i, gm_off // D + 1)),
            pl.BlockSpec((tm, D), lambda i: (i, 0)),
            pl.BlockSpec(conv_w.shape, lambda i: (0, 0)),
            pl.BlockSpec(w_bm.shape, lambda i: (0, 0), **const),
            pl.BlockSpec(w_bc.shape, lambda i: (0, 0), **const),
            pl.BlockSpec(w_out.shape, lambda i: (0, 0), **const),
            pl.BlockSpec((1, D), lambda i: (0, 0)),
        ],
        out_specs=[
            pl.BlockSpec((tm, D), lambda i: (i, 0)),
            pl.BlockSpec((tm, D), lambda i: (i, 0)),
        ],
        out_shape=[
            jax.ShapeDtypeStruct((M, D), F32),
            jax.ShapeDtypeStruct((M, D), BF16),
        ],
        scratch_shapes=[pltpu.VMEM((tm, D), BF16)],
        compiler_params=_params(1),
        name="mix",
    )(hm, proj, proj, proj, proj, x2, conv_w, w_bm, w_bc, w_out, g2)


def _ffn_kernel(hn_ref, h1_ref, wu_ref, cw_ref, cb_ref, wd_ref, g3_ref, out_ref,
                u_ref, a_ref, carry_ref, *, nj, blocks_per_seq, norm_chunk=64, sub=512):
    s = pl.program_id(0)
    tm, tn = hn_ref.shape[0], wd_ref.shape[0]
    tile_b = jnp.maximum(s - 1, 0)
    ib, jb = tile_b // nj, tile_b % nj
    jc = (s + nj - 2) % nj

    @pl.when(s == 0)
    def _():
        u_ref[...] = jnp.zeros_like(u_ref)
        a_ref[...] = jnp.zeros_like(a_ref)
        carry_ref[...] = jnp.zeros_like(carry_ref)

    @pl.when((jc == 0) | (s == 0))
    def _():
        out_ref[...] = h1_ref[...]

    prevs = [jnp.where(ib % blocks_per_seq == 0, 0.0, carry_ref[jb])]
    prevs += [u_ref[h * sub - HALO:h * sub, :] for h in range(1, tm // sub)]
    carry_ref[jb] = u_ref[tm - HALO:tm, :]

    for h in range(tm // sub):
        rows = slice(h * sub, (h + 1) * sub)
        out_ref[rows, :] += jnp.dot(a_ref[rows, :], wd_ref[...], preferred_element_type=F32)
        y = _causal_conv3(u_ref[rows, :], prevs[h], cw_ref) + cb_ref[...]
        gate, val = y[:, :tn], y[:, tn:]
        a_ref[rows, :] = (gate * _sigmoid(gate) * val).astype(BF16)
        u_ref[rows, :] = jnp.dot(hn_ref[rows, :], wu_ref[...], preferred_element_type=F32)

    @pl.when((jc == nj - 1) & (s >= 2))
    def _():
        def body(r, carry):
            rows = pl.ds(pl.multiple_of(r * norm_chunk, norm_chunk), norm_chunk)
            h2 = out_ref[rows, :]
            rs = lax.rsqrt(jnp.mean(h2 * h2, axis=-1, keepdims=True) + EPS)
            out_ref[rows, :] = h2 * rs * g3_ref[...]
            return carry
        lax.fori_loop(0, tm // norm_chunk, body, 0, unroll=True)


def _ffn(hn2, h1, w_up_t, cw_t, cb_t, w_down_p, g3, *, seq, tm=512):
    M, D = h1.shape
    nj, _, tn2 = w_up_t.shape
    tn = tn2 // 2
    assert M % tm == 0 and seq % tm == 0 and w_down_p.shape[0] == nj * tn
    ni = M // tm
    n_tiles = ni * nj
    row_a = lambda s: (jnp.minimum(s // nj, ni - 1), 0)
    row_c = lambda s: (jnp.maximum(s - 2, 0) // nj, 0)
    tile_a = lambda s: (s % nj, 0, 0)
    tile_b = lambda s: (jnp.maximum(s - 1, 0) % nj, 0, 0)
    return pl.pallas_call(
        functools.partial(_ffn_kernel, nj=nj, blocks_per_seq=seq // tm),
        grid=(n_tiles + 2,),
        in_specs=[
            pl.BlockSpec((tm, D), row_a),
            pl.BlockSpec((tm, D), row_c),
            pl.BlockSpec((None, D, tn2), tile_a),
            pl.BlockSpec((None, 3, tn2), tile_b),
            pl.BlockSpec((None, 1, tn2), tile_b),
            pl.BlockSpec((tn, D), lambda s: ((s + nj - 2) % nj, 0)),
            pl.BlockSpec((1, D), lambda s: (0, 0)),
        ],
        out_specs=pl.BlockSpec((tm, D), row_c),
        out_shape=jax.ShapeDtypeStruct((M, D), F32),
        scratch_shapes=[
            pltpu.VMEM((tm, tn2), F32),
            pltpu.VMEM((tm, tn), BF16),
            pltpu.VMEM((nj, HALO, tn2), F32),
        ],
        compiler_params=_params(1),
        name="ffn",
    )(hn2, h1, w_up_t, cw_t, cb_t, w_down_p, g3)


def _win_prep_kernel(w_ref, tail_ref, gate_ref, o_ref, og_ref, *, first_shifted):
    i = pl.program_id(0)
    m = w_ref[...]
    shifted = jnp.concatenate([m[GATE_COLS:, :], tail_ref[...]], axis=0)
    o_ref[...] = jnp.where(i >= first_shifted, shifted, m).astype(BF16)

    gr = gate_ref[...]
    lead = lax.broadcasted_iota(jnp.int32, gr.shape, 0) < M_HEADS
    zeros = jnp.zeros((LANES - GATE_COLS, gr.shape[1]), F32)
    og_ref[...] = jnp.concatenate(
        [jnp.where(lead, gr, 0.0), zeros,
         jnp.where(lead, pltpu.roll(gr, M_HEADS, 0), 0.0), zeros], axis=0).astype(BF16)


def _win_prep(w_in_t, *, rows=512):
    n, D = w_in_t.shape
    n_main = n - GATE_COLS
    assert n_main % rows == 0 and MAIN_SPLIT % rows == 0 and rows % GATE_COLS == 0
    return pl.pallas_call(
        functools.partial(_win_prep_kernel, first_shifted=MAIN_SPLIT // rows),
        grid=(n_main // rows,),
        in_specs=[
            pl.BlockSpec((rows, D), lambda i: (i, 0)),
            pl.BlockSpec((GATE_COLS, D), lambda i: ((i + 1) * (rows // GATE_COLS), 0)),
            pl.BlockSpec((GATE_COLS, D), lambda i: (MAIN_SPLIT // GATE_COLS, 0)),
        ],
        out_specs=[
            pl.BlockSpec((rows, D), lambda i: (i, 0)),
            pl.BlockSpec((2 * LANES, D), lambda i: (0, 0)),
        ],
        out_shape=[
            jax.ShapeDtypeStruct((n_main, D), BF16),
            jax.ShapeDtypeStruct((2 * LANES, D), BF16),
        ],
        compiler_params=_params(1),
        name="win_prep",
    )(w_in_t, w_in_t, w_in_t)


def _side_casts(w_up, w_down, w_out, w_branch_m, w_branch_c, *, d_ff, tn, chunks=64, down_rows=LANES):
    D = w_up.shape[0]
    nj = pl.cdiv(d_ff, tn)
    ffp = nj * tn
    assert d_ff % LANES == 0 and tn % LANES == 0 and ffp % down_rows == 0

    def rows_cast(w):
        r = w.shape[0] // chunks
        assert r * chunks == w.shape[0] and r % 16 == 0
        return (w, _cast_rows, (r, w.shape[1]), (r, w.shape[1]), w.shape, chunks, chunks)

    up_rows = D // chunks
    assert up_rows * chunks == D and up_rows % 16 == 0
    return [
        (w_up, _cast_tile_pairs(d_ff, tn), (up_rows, 2 * d_ff), (nj, up_rows, 2 * tn),
         (nj, D, 2 * tn), chunks, chunks),
        (w_down, _cast_rows_zero_tail(d_ff), (down_rows, D), (down_rows, D), (ffp, D),
         pl.cdiv(d_ff, down_rows), ffp // down_rows),
        rows_cast(w_out), rows_cast(w_branch_m), rows_cast(w_branch_c),
    ]


def _tile_pairs(a, d_ff, tn):
    nj = pl.cdiv(d_ff, tn)
    pad = lambda h: jnp.pad(h, ((0, 0), (0, nj * tn - d_ff))).reshape(a.shape[0], nj, tn)
    both = jnp.concatenate([pad(a[:, :d_ff]), pad(a[:, d_ff:])], axis=2)
    return jnp.transpose(both, (1, 0, 2))


def kernel(x, norm_mix_g, w_in, b_in, mlstm_head_g, w_branch_m, conv_w, w_branch_c, w_out,
           norm_ffn_g, w_up, ffn_conv_w, ffn_conv_b, w_down, norm_out_g):
    B, T, D = x.shape
    M = B * T
    d_ff = w_down.shape[0]
    ff_tile = 512
    ffp = pl.cdiv(d_ff, ff_tile) * ff_tile
    x2 = x.reshape(M, D)

    g0, g1 = MAIN_SPLIT, MAIN_SPLIT + GATE_COLS
    gi = g0 + M_HEADS
    w_in_t = w_in.T
    w_main_t, w_gate_t = _win_prep(w_in_t)
    lead = lambda a: jnp.pad(a, ((0, LANES - M_HEADS), (0, 0)))
    b_main = jnp.concatenate([b_in[:g0], b_in[g1:]])[None, :]
    b_gate = jnp.concatenate([lead(b_in[g0:gi, None]), lead(b_in[gi:g1, None])], axis=0).T
    cw_t = _tile_pairs(ffn_conv_w, d_ff, ff_tile)
    cb_t = _tile_pairs(ffn_conv_b[None, :], d_ff, ff_tile)

    side = _side_casts(w_up, w_down, w_out, w_branch_m, w_branch_c, d_ff=d_ff, tn=ff_tile)
    proj, gates, (w_up_t, w_down_p, w_out_b, w_bm_b, w_bc_b) = _inproj(
        x2, norm_mix_g[None, :], w_main_t, b_main, w_gate_t, b_gate, side)
    hm = _mlstm(proj, gates, mlstm_head_g[None, :], batch=B, seq=T)
    h1, hn2 = _mix(hm, proj, x2, conv_w, w_bm_b, w_bc_b, w_out_b, norm_ffn_g[None, :], seq=T)
    out = _ffn(hn2, h1, w_up_t, cw_t, cb_t, w_down_p, norm_out_g[None, :], seq=T)
    return out.reshape(B, T, D)
```

```python
import functools
import math

import jax
import jax.numpy as jnp
from jax import lax
from jax.experimental import pallas as pl
from jax.experimental.pallas import tpu as pltpu

F32 = jnp.float32
BF16 = jnp.bfloat16

EPS = 1e-6
M_HEADS = 4
M_DV = 256
M_DQK = 128
M_WIDTH = M_HEADS * M_DV
M_QK_WIDTH = M_HEADS * M_DQK
C_WIDTH = 1024
LOG_QK_SCALE = -0.5 * math.log(M_DQK)

LANES = 128
GATE_COLS = 2 * M_HEADS
MAIN_SPLIT = 2 * M_QK_WIDTH + 2 * M_WIDTH

VMEM_LIMIT = 56 * 1024 * 1024


def _params(n_axes, vmem=VMEM_LIMIT, flags=None):
    return pltpu.CompilerParams(dimension_semantics=("arbitrary",) * n_axes,
                                vmem_limit_bytes=vmem, flags=flags)


def _sigmoid(v):
    return 1.0 / (1.0 + jnp.exp(-v))


def _inproj_kernel(x_ref, g_ref, w_ref, b_ref, wg_ref, bg_ref, *rest, row_chunk, side):
    n_side = len(side)
    side_in = rest[:n_side]
    proj_ref, gates_ref = rest[n_side:n_side + 2]
    side_out = rest[n_side + 2:2 * n_side + 2]
    hn_ref = rest[-1]
    j = pl.program_id(1)
    tm = x_ref.shape[0]
    step = pl.program_id(0) * pl.num_programs(1) + j

    @pl.when(j == 0)
    def _():
        def body(r, carry):
            rows = pl.ds(pl.multiple_of(r * row_chunk, row_chunk), row_chunk)
            xb = x_ref[rows, :]
            ms = jnp.mean(xb * xb, axis=-1, keepdims=True)
            hn_ref[rows, :] = (xb * lax.rsqrt(ms + EPS) * g_ref[...]).astype(BF16)
            return carry
        lax.fori_loop(0, tm // row_chunk, body, 0, unroll=True)
        gates_ref[...] = _dot_nt(hn_ref[...], wg_ref[...]) + bg_ref[...]

    proj_ref[...] = (_dot_nt(hn_ref[...], w_ref[...]) + b_ref[...]).astype(proj_ref.dtype)
    for (relayout, n_chunks), i_ref, o_ref in zip(side, side_in, side_out):
        relayout(i_ref, o_ref, jnp.minimum(step, n_chunks - 1))


def _dot_nt(a, b_t):
    return lax.dot_general(a, b_t, (((1,), (1,)), ((), ())), preferred_element_type=F32)


def _cast_rows(i_ref, o_ref, chunk):
    o_ref[...] = i_ref[...].astype(BF16)


def _cast_rows_zero_tail(valid_rows):
    def relayout(i_ref, o_ref, chunk):
        row = chunk * i_ref.shape[0] + lax.broadcasted_iota(jnp.int32, i_ref.shape, 0)
        o_ref[...] = jnp.where(row < valid_rows, i_ref[...], 0.0).astype(BF16)
    return relayout


def _cast_tile_pairs(d_ff, tn):
    def relayout(i_ref, o_ref, chunk):
        for t in range(o_ref.shape[0]):
            lo = t * tn
            width = min(tn, d_ff - lo)
            o_ref[t, :, 0:width] = i_ref[:, lo:lo + width].astype(BF16)
            o_ref[t, :, tn:tn + width] = i_ref[:, d_ff + lo:d_ff + lo + width].astype(BF16)
            if width < tn:
                pad = jnp.zeros((o_ref.shape[1], tn - width), BF16)
                o_ref[t, :, width:tn] = pad
                o_ref[t, :, tn + width:2 * tn] = pad
    return relayout


def _inproj(x2, g, w_main_t, b_main, w_gate_t, b_gate, side, *, tm=1024, tn=1024):
    M, D = x2.shape
    N = w_main_t.shape[0]
    assert M % tm == 0 and N % tn == 0 and w_gate_t.shape == (2 * LANES, D)
    nj = N // tn
    n_steps = (M // tm) * nj

    def chunk_map(n_chunks, axis, rank):
        def index_map(i, j):
            idx = [0] * rank
            idx[axis] = jnp.minimum(i * nj + j, n_chunks - 1)
            return tuple(idx)
        return index_map

    side_in_specs, side_out_specs, side_out_shapes, side_meta = [], [], [], []
    for arr, relayout, in_block, out_block, out_shape, n_in, n_out in side:
        assert n_out <= n_steps and n_in <= n_out
        side_in_specs.append(pl.BlockSpec(in_block, chunk_map(n_in, 0, 2)))
        side_out_specs.append(pl.BlockSpec(out_block, chunk_map(n_out, len(out_block) - 2,
                                                                len(out_block))))
        side_out_shapes.append(jax.ShapeDtypeStruct(out_shape, BF16))
        side_meta.append((relayout, n_out))

    outs = pl.pallas_call(
        functools.partial(_inproj_kernel, row_chunk=128, side=tuple(side_meta)),
        grid=(M // tm, nj),
        in_specs=[
            pl.BlockSpec((tm, D), lambda i, j: (i, 0)),
            pl.BlockSpec((1, D), lambda i, j: (0, 0)),
            pl.BlockSpec((tn, D), lambda i, j: (j, 0)),
            pl.BlockSpec((1, tn), lambda i, j: (0, j)),
            pl.BlockSpec((2 * LANES, D), lambda i, j: (0, 0)),
            pl.BlockSpec((1, 2 * LANES), lambda i, j: (0, 0)),
        ] + side_in_specs,
        out_specs=[
            pl.BlockSpec((tm, tn), lambda i, j: (i, j)),
            pl.BlockSpec((tm, 2 * LANES), lambda i, j: (i, 0)),
        ] + side_out_specs,
        out_shape=[
            jax.ShapeDtypeStruct((M, N), BF16),
            jax.ShapeDtypeStruct((M, 2 * LANES), F32),
        ] + side_out_shapes,
        scratch_shapes=[pltpu.VMEM((tm, D), BF16)],
        compiler_params=_params(2),
        name="inproj",
    )(x2, g, w_main_t, b_main, w_gate_t, b_gate, *[s[0] for s in side])
    return outs[0], outs[1], outs[2:]


def _split3(a):
    hi = a.astype(BF16)
    r1 = a - hi.astype(F32)
    mid = r1.astype(BF16)
    lo = (r1 - mid.astype(F32)).astype(BF16)
    return hi, mid, lo


def _mlstm_kernel(p_ref, g_ref, hg_ref, out_ref, st_ref, m_ref):
    c = pl.program_id(0)

    @pl.when(c == 0)
    def _():
        st_ref[...] = jnp.zeros_like(st_ref)
        m_ref[...] = jnp.zeros_like(m_ref)

    for seq in range(p_ref.shape[0]):
        _mlstm_chunk(p_ref.at[seq], g_ref.at[seq], hg_ref, out_ref.at[seq], st_ref.at[seq],
                     m_ref.at[seq])


def _mlstm_chunk(p_ref, g_ref, hg_ref, out_ref, st_ref, m_ref):
    L = p_ref.shape[0]

    logi = g_ref[:, :LANES]
    f_pre = g_ref[:, LANES:]
    logf = jnp.minimum(f_pre, 0.0) - jnp.log1p(jnp.exp(-jnp.abs(f_pre)))

    row = lax.broadcasted_iota(jnp.int32, (L, L), 0)
    col = lax.broadcasted_iota(jnp.int32, (L, L), 1)
    causal = row >= col
    tri = causal.astype(BF16)
    parts = _split3(logf)
    b = (jnp.dot(tri, parts[2], preferred_element_type=F32)
         + jnp.dot(tri, parts[1], preferred_element_type=F32)
         + jnp.dot(tri, parts[0], preferred_element_type=F32))
    g = logi - b

    trow = lax.broadcasted_iota(jnp.int32, g.shape, 0)
    run = g
    shift = 1
    while shift < L:
        run = jnp.maximum(run, jnp.where(trow >= shift, pltpu.roll(run, shift, 0), -jnp.inf))
        shift *= 2
    m_prev_row = m_ref[0:1, :]
    big_m = jnp.maximum(run, m_prev_row)
    m_last = big_m[L - 1:L, :]
    m_ref[...] = jnp.broadcast_to(b[L - 1:L, :] + m_last, m_ref.shape)
    e_col = jnp.exp(-(b + big_m))
    g_t = g.T

    ones_blk = jnp.ones((L, LANES), BF16)
    for h in range(M_HEADS):
        q = p_ref[:, h * M_DQK:(h + 1) * M_DQK]
        k = p_ref[:, M_QK_WIDTH + h * M_DQK:M_QK_WIDTH + (h + 1) * M_DQK]
        v = p_ref[:, 2 * M_QK_WIDTH + h * M_DV:2 * M_QK_WIDTH + (h + 1) * M_DV]
        o_pre = p_ref[:, 2 * M_QK_WIDTH + M_WIDTH + h * M_DV:
                      2 * M_QK_WIDTH + M_WIDTH + (h + 1) * M_DV].astype(F32)
        v_ext = jnp.concatenate([v, ones_blk], axis=1)

        g_row = g_t[h:h + 1, :]
        m_rep = jnp.broadcast_to(big_m[:, h:h + 1], (L, LANES))
        m_prev = m_prev_row[:, h:h + 1]
        m_end = m_last[:, h:h + 1]

        arg = (g_row + LOG_QK_SCALE) - jnp.concatenate([m_rep] * (L // LANES), axis=1)
        qk = lax.dot_general(q, k, (((1,), (1,)), ((), ())), preferred_element_type=F32)
        S = (qk * jnp.exp(jnp.where(causal, arg, -jnp.inf))).astype(BF16)
        intra = jnp.dot(S, v_ext, preferred_element_type=F32)
        Ct = st_ref[h]
        inter = jnp.dot(q, Ct.astype(BF16), preferred_element_type=F32)
        isc = jnp.exp((m_prev + LOG_QK_SCALE) - m_rep)
        tot = intra + jnp.concatenate([isc] * (intra.shape[1] // LANES), axis=1) * inter
        num = tot[:, :M_DV]

        den = tot[:, M_DV:M_DV + 1]
        f = 1.0 / jnp.maximum(jnp.abs(den), e_col[:, h:h + 1])
        ms = jnp.mean(num * num, axis=-1, keepdims=True)
        fac = f * lax.rsqrt(ms * (f * f) + EPS)
        hh = num * fac * hg_ref[:, h * M_DV:(h + 1) * M_DV]
        out_ref[:, h * M_DV:(h + 1) * M_DV] = (hh * _sigmoid(o_pre)).astype(out_ref.dtype)

        kw_t = (k.T.astype(F32) * jnp.exp(g_row - m_end)).astype(BF16)
        upd = jnp.dot(kw_t, v_ext, preferred_element_type=F32)
        st_ref[h] = jnp.exp(m_prev - m_end) * Ct + upd


def _mlstm(proj, gates, head_g, *, batch, seq, chunk=256):
    M = proj.shape[0]
    nc = seq // chunk
    assert seq % chunk == 0 and M == batch * seq and chunk % LANES == 0
    width = 2 * M_QK_WIDTH + 2 * M_WIDTH
    hm = pl.pallas_call(
        _mlstm_kernel,
        grid=(nc,),
        in_specs=[
            pl.BlockSpec((batch, chunk, width), lambda c: (0, c, 0)),
            pl.BlockSpec((batch, chunk, 2 * LANES), lambda c: (0, c, 0)),
            pl.BlockSpec((1, M_WIDTH), lambda c: (0, 0)),
        ],
        out_specs=pl.BlockSpec((batch, chunk, M_WIDTH), lambda c: (0, c, 0)),
        out_shape=jax.ShapeDtypeStruct((batch, seq, M_WIDTH), BF16),
        scratch_shapes=[
            pltpu.VMEM((batch, M_HEADS, M_DQK, M_DV + LANES), F32),
            pltpu.VMEM((batch, 8, LANES), F32),
        ],
        compiler_params=_params(1),
        name="mlstm",
    )(proj.reshape(batch, seq, -1), gates.reshape(batch, seq, -1), head_g)
    return hm.reshape(M, M_WIDTH)


HALO = 8


def _causal_conv3(cur, prev, w_ref):
    n = prev.shape[0]
    ext = jnp.concatenate([prev, cur], axis=0)
    r1 = pltpu.roll(ext, 1, 0)[n:, :]
    r2 = pltpu.roll(ext, 2, 0)[n:, :]
    return r2 * w_ref[0:1, :] + r1 * w_ref[1:2, :] + cur * w_ref[2:3, :]


def _mix_kernel(hm_ref, cb_ref, halo_ref, gm_ref, gc_ref, x_ref, cw_ref, wbm_ref, wbc_ref,
                wout_ref, g2_ref, h1_ref, hn2_ref, merged_ref, *, blocks_per_seq, col_chunk):
    i = pl.program_id(0)
    tm = hm_ref.shape[0]
    D = x_ref.shape[1]

    cb = cb_ref[:, 0:C_WIDTH].astype(F32)
    pp = cb_ref[:, C_WIDTH:2 * C_WIDTH].astype(F32) * cb_ref[:, 2 * C_WIDTH:3 * C_WIDTH].astype(F32)
    hp = halo_ref[:, 0:C_WIDTH].astype(F32) * halo_ref[:, C_WIDTH:2 * C_WIDTH].astype(F32)
    hp = jnp.where(i % blocks_per_seq != 0, hp, 0.0)
    z = (cb * _causal_conv3(pp, hp[halo_ref.shape[0] - HALO:, :], cw_ref)).astype(BF16)
    hm = hm_ref[...]

    for n in range(D // col_chunk):
        cols = slice(n * col_chunk, (n + 1) * col_chunk)
        y_m = jnp.dot(hm, wbm_ref[:, cols], preferred_element_type=F32)
        y_c = jnp.dot(z, wbc_ref[:, cols], preferred_element_type=F32)
        merged_ref[:, cols] = (_sigmoid(gm_ref[:, cols].astype(F32)) * y_m
                               + _sigmoid(gc_ref[:, cols].astype(F32)) * y_c).astype(BF16)

    ssq = jnp.zeros((tm, 1), F32)
    for n in range(D // col_chunk):
        cols = slice(n * col_chunk, (n + 1) * col_chunk)
        h1 = x_ref[:, cols] + jnp.dot(merged_ref[...], wout_ref[:, cols],
                                      preferred_element_type=F32)
        h1_ref[:, cols] = h1
        ssq = ssq + jnp.sum(h1 * h1, axis=-1, keepdims=True)
    rs = lax.rsqrt(ssq / D + EPS)
    for n in range(D // col_chunk):
        cols = slice(n * col_chunk, (n + 1) * col_chunk)
        hn2_ref[:, cols] = (h1_ref[:, cols] * rs * g2_ref[:, cols]).astype(hn2_ref.dtype)


def _mix(hm, proj, x2, conv_w, w_bm, w_bc, w_out, g2, *, seq, tm=256, halo=16):
    M, D = x2.shape
    assert M % tm == 0 and seq % tm == 0 and tm % halo == 0
    cb_blk = MAIN_SPLIT // (3 * C_WIDTH)
    assert cb_blk * 3 * C_WIDTH == MAIN_SPLIT
    cc_off = MAIN_SPLIT + C_WIDTH
    gm_off = MAIN_SPLIT + 3 * C_WIDTH
    assert cc_off % (2 * C_WIDTH) == 0 and gm_off % D == 0
    const = dict(pipeline_mode=pl.Buffered(1))
    return pl.pallas_call(
        functools.partial(_mix_kernel, blocks_per_seq=seq // tm, col_chunk=512),
        grid=(M // tm,),
        in_specs=[
            pl.BlockSpec((tm, M_WIDTH), lambda i: (i, 0)),
            pl.BlockSpec((tm, 3 * C_WIDTH), lambda i: (i, cb_blk)),
            pl.BlockSpec((halo, 2 * C_WIDTH),
                         lambda i: (jnp.maximum(i * (tm // halo) - 1, 0), cc_off // (2 * C_WIDTH))),
            pl.BlockSpec((tm, D), lambda i: (i, gm_off // D)),
            pl.BlockSpec((tm, D), lambda i: (i, gm_off // D + 1)),
            pl.BlockSpec((tm, D), lambda i: (i, 0)),
            pl.BlockSpec(conv_w.shape, lambda i: (0, 0)),
            pl.BlockSpec(w_bm.shape, lambda i: (0, 0), **const),
            pl.BlockSpec(w_bc.shape, lambda i: (0, 0), **const),
            pl.BlockSpec(w_out.shape, lambda i: (0, 0), **const),
            pl.BlockSpec((1, D), lambda i: (0, 0)),
        ],
        out_specs=[
            pl.BlockSpec((tm, D), lambda i: (i, 0)),
            pl.BlockSpec((tm, D), lambda i: (i, 0)),
        ],
        out_shape=[
            jax.ShapeDtypeStruct((M, D), F32),
            jax.ShapeDtypeStruct((M, D), BF16),
        ],
        scratch_shapes=[pltpu.VMEM((tm, D), BF16)],
        compiler_params=_params(1),
        name="mix",
    )(hm, proj, proj, proj, proj, x2, conv_w, w_bm, w_bc, w_out, g2)


def _ffn_kernel(hn_ref, h1_ref, wu_ref, cw_ref, cb_ref, wd_ref, g3_ref, out_ref,
                u_ref, a_ref, carry_ref, *, nj, blocks_per_seq, norm_chunk=64, sub=512):
    s = pl.program_id(0)
    tm, tn = hn_ref.shape[0], wd_ref.shape[0]
    tile_b = jnp.maximum(s - 1, 0)
    ib, jb = tile_b // nj, tile_b % nj
    jc = (s + nj - 2) % nj

    @pl.when(s == 0)
    def _():
        u_ref[...] = jnp.zeros_like(u_ref)
        a_ref[...] = jnp.zeros_like(a_ref)
        carry_ref[...] = jnp.zeros_like(carry_ref)

    @pl.when((jc == 0) | (s == 0))
    def _():
        out_ref[...] = h1_ref[...]

    prevs = [jnp.where(ib % blocks_per_seq == 0, 0.0, carry_ref[jb])]
    prevs += [u_ref[h * sub - HALO:h * sub, :] for h in range(1, tm // sub)]
    carry_ref[jb] = u_ref[tm - HALO:tm, :]

    for h in range(tm // sub):
        rows = slice(h * sub, (h + 1) * sub)
        out_ref[rows, :] += jnp.dot(a_ref[rows, :], wd_ref[...], preferred_element_type=F32)
        y = _causal_conv3(u_ref[rows, :], prevs[h], cw_ref) + cb_ref[...]
        gate, val = y[:, :tn], y[:, tn:]
        a_ref[rows, :] = (gate * _sigmoid(gate) * val).astype(BF16)
        u_ref[rows, :] = jnp.dot(hn_ref[rows, :], wu_ref[...], preferred_element_type=F32)

    @pl.when((jc == nj - 1) & (s >= 2))
    def _():
        def body(r, carry):
            rows = pl.ds(pl.multiple_of(r * norm_chunk, norm_chunk), norm_chunk)
            h2 = out_ref[rows, :]
            rs = lax.rsqrt(jnp.mean(h2 * h2, axis=-1, keepdims=True) + EPS)
            out_ref[rows, :] = h2 * rs * g3_ref[...]
            return carry
        lax.fori_loop(0, tm // norm_chunk, body, 0, unroll=True)


def _ffn(hn2, h1, w_up_t, cw_t, cb_t, w_down_p, g3, *, seq, tm=512):
    M, D = h1.shape
    nj, _, tn2 = w_up_t.shape
    tn = tn2 // 2
    assert M % tm == 0 and seq % tm == 0 and w_down_p.shape[0] == nj * tn
    ni = M // tm
    n_tiles = ni * nj
    row_a = lambda s: (jnp.minimum(s // nj, ni - 1), 0)
    row_c = lambda s: (jnp.maximum(s - 2, 0) // nj, 0)
    tile_a = lambda s: (s % nj, 0, 0)
    tile_b = lambda s: (jnp.maximum(s - 1, 0) % nj, 0, 0)
    return pl.pallas_call(
        functools.partial(_ffn_kernel, nj=nj, blocks_per_seq=seq // tm),
        grid=(n_tiles + 2,),
        in_specs=[
            pl.BlockSpec((tm, D), row_a),
            pl.BlockSpec((tm, D), row_c),
            pl.BlockSpec((None, D, tn2), tile_a),
            pl.BlockSpec((None, 3, tn2), tile_b),
            pl.BlockSpec((None, 1, tn2), tile_b),
            pl.BlockSpec((tn, D), lambda s: ((s + nj - 2) % nj, 0)),
            pl.BlockSpec((1, D), lambda s: (0, 0)),
        ],
        out_specs=pl.BlockSpec((tm, D), row_c),
        out_shape=jax.ShapeDtypeStruct((M, D), F32),
        scratch_shapes=[
            pltpu.VMEM((tm, tn2), F32),
            pltpu.VMEM((tm, tn), BF16),
            pltpu.VMEM((nj, HALO, tn2), F32),
        ],
        compiler_params=_params(1),
        name="ffn",
    )(hn2, h1, w_up_t, cw_t, cb_t, w_down_p, g3)


def _win_prep_kernel(w_ref, tail_ref, gate_ref, o_ref, og_ref, *, first_shifted):
    i = pl.program_id(0)
    m = w_ref[...]
    shifted = jnp.concatenate([m[GATE_COLS:, :], tail_ref[...]], axis=0)
    o_ref[...] = jnp.where(i >= first_shifted, shifted, m).astype(BF16)

    gr = gate_ref[...]
    lead = lax.broadcasted_iota(jnp.int32, gr.shape, 0) < M_HEADS
    zeros = jnp.zeros((LANES - GATE_COLS, gr.shape[1]), F32)
    og_ref[...] = jnp.concatenate(
        [jnp.where(lead, gr, 0.0), zeros,
         jnp.where(lead, pltpu.roll(gr, M_HEADS, 0), 0.0), zeros], axis=0).astype(BF16)


def _win_prep(w_in_t, *, rows=512):
    n, D = w_in_t.shape
    n_main = n - GATE_COLS
    assert n_main % rows == 0 and MAIN_SPLIT % rows == 0 and rows % GATE_COLS == 0
    return pl.pallas_call(
        functools.partial(_win_prep_kernel, first_shifted=MAIN_SPLIT // rows),
        grid=(n_main // rows,),
        in_specs=[
            pl.BlockSpec((rows, D), lambda i: (i, 0)),
            pl.BlockSpec((GATE_COLS, D), lambda i: ((i + 1) * (rows // GATE_COLS), 0)),
            pl.BlockSpec((GATE_COLS, D), lambda i: (MAIN_SPLIT // GATE_COLS, 0)),
        ],
        out_specs=[
            pl.BlockSpec((rows, D), lambda i: (i, 0)),
            pl.BlockSpec((2 * LANES, D), lambda i: (0, 0)),
        ],
        out_shape=[
            jax.ShapeDtypeStruct((n_main, D), BF16),
            jax.ShapeDtypeStruct((2 * LANES, D), BF16),
        ],
        compiler_params=_params(1),
        name="win_prep",
    )(w_in_t, w_in_t, w_in_t)


def _side_casts(w_up, w_down, w_out, w_branch_m, w_branch_c, *, d_ff, tn, chunks=64, down_rows=LANES):
    D = w_up.shape[0]
    nj = pl.cdiv(d_ff, tn)
    ffp = nj * tn
    assert d_ff % LANES == 0 and tn % LANES == 0 and ffp % down_rows == 0

    def rows_cast(w):
        r = w.shape[0] // chunks
        assert r * chunks == w.shape[0] and r % 16 == 0
        return (w, _cast_rows, (r, w.shape[1]), (r, w.shape[1]), w.shape, chunks, chunks)

    up_rows = D // chunks
    assert up_rows * chunks == D and up_rows % 16 == 0
    return [
        (w_up, _cast_tile_pairs(d_ff, tn), (up_rows, 2 * d_ff), (nj, up_rows, 2 * tn),
         (nj, D, 2 * tn), chunks, chunks),
        (w_down, _cast_rows_zero_tail(d_ff), (down_rows, D), (down_rows, D), (ffp, D),
         pl.cdiv(d_ff, down_rows), ffp // down_rows),
        rows_cast(w_out), rows_cast(w_branch_m), rows_cast(w_branch_c),
    ]


def _tile_pairs(a, d_ff, tn):
    nj = pl.cdiv(d_ff, tn)
    pad = lambda h: jnp.pad(h, ((0, 0), (0, nj * tn - d_ff))).reshape(a.shape[0], nj, tn)
    both = jnp.concatenate([pad(a[:, :d_ff]), pad(a[:, d_ff:])], axis=2)
    return jnp.transpose(both, (1, 0, 2))


def kernel(x, norm_mix_g, w_in, b_in, mlstm_head_g, w_branch_m, conv_w, w_branch_c, w_out,
           norm_ffn_g, w_up, ffn_conv_w, ffn_conv_b, w_down, norm_out_g):
    B, T, D = x.shape
    M = B * T
    d_ff = w_down.shape[0]
    ff_tile = 512
    ffp = pl.cdiv(d_ff, ff_tile) * ff_tile
    x2 = x.reshape(M, D)

    g0, g1 = MAIN_SPLIT, MAIN_SPLIT + GATE_COLS
    gi = g0 + M_HEADS
    w_in_t = w_in.T
    w_main_t, w_gate_t = _win_prep(w_in_t)
    lead = lambda a: jnp.pad(a, ((0, LANES - M_HEADS), (0, 0)))
    b_main = jnp.concatenate([b_in[:g0], b_in[g1:]])[None, :]
    b_gate = jnp.concatenate([lead(b_in[g0:gi, None]), lead(b_in[gi:g1, None])], axis=0).T
    cw_t = _tile_pairs(ffn_conv_w, d_ff, ff_tile)
    cb_t = _tile_pairs(ffn_conv_b[None, :], d_ff, ff_tile)

    side = _side_casts(w_up, w_down, w_out, w_branch_m, w_branch_c, d_ff=d_ff, tn=ff_tile)
    proj, gates, (w_up_t, w_down_p, w_out_b, w_bm_b, w_bc_b) = _inproj(
        x2, norm_mix_g[None, :], w_main_t, b_main, w_gate_t, b_gate, side)
    hm = _mlstm(proj, gates, mlstm_head_g[None, :], batch=B, seq=T)
    h1, hn2 = _mix(hm, proj, x2, conv_w, w_bm_b, w_bc_b, w_out_b, norm_ffn_g[None, :], seq=T)
    out = _ffn(hn2, h1, w_up_t, cw_t, cb_t, w_down_p, norm_out_g[None, :], seq=T)
    return out.reshape(B, T, D)
```

```python
import functools
import math

import jax
import jax.numpy as jnp
from jax import lax
from jax.experimental import pallas as pl
from jax.experimental.pallas import tpu as pltpu

F32 = jnp.float32
BF16 = jnp.bfloat16

EPS = 1e-6
M_HEADS = 4
M_DV = 256
M_DQK = 128
M_WIDTH = M_HEADS * M_DV
M_QK_WIDTH = M_HEADS * M_DQK
C_WIDTH = 1024
LOG_QK_SCALE = -0.5 * math.log(M_DQK)

LANES = 128
GATE_COLS = 2 * M_HEADS
MAIN_SPLIT = 2 * M_QK_WIDTH + 2 * M_WIDTH

VMEM_LIMIT = 56 * 1024 * 1024


def _params(n_axes):
    return pltpu.CompilerParams(dimension_semantics=("arbitrary",) * n_axes,
                                vmem_limit_bytes=VMEM_LIMIT)


def _sigmoid(v):
    return 1.0 / (1.0 + jnp.exp(-v))


def _inproj_kernel(x_ref, g_ref, w_ref, b_ref, wg_ref, bg_ref, *rest, row_chunk, side):
    n_side = len(side)
    side_in = rest[:n_side]
    proj_ref, gates_ref = rest[n_side:n_side + 2]
    side_out = rest[n_side + 2:2 * n_side + 2]
    hn_ref = rest[-1]
    j = pl.program_id(1)
    tm = x_ref.shape[0]
    step = pl.program_id(0) * pl.num_programs(1) + j

    @pl.when(j == 0)
    def _():
        def body(r, carry):
            rows = pl.ds(pl.multiple_of(r * row_chunk, row_chunk), row_chunk)
            xb = x_ref[rows, :]
            ms = jnp.mean(xb * xb, axis=-1, keepdims=True)
            hn_ref[rows, :] = (xb * lax.rsqrt(ms + EPS) * g_ref[...]).astype(BF16)
            return carry
        lax.fori_loop(0, tm // row_chunk, body, 0, unroll=True)
        gates_ref[...] = _dot_nt(hn_ref[...], wg_ref[...]) + bg_ref[...]

    proj_ref[...] = (_dot_nt(hn_ref[...], w_ref[...]) + b_ref[...]).astype(proj_ref.dtype)
    for (relayout, n_chunks), i_ref, o_ref in zip(side, side_in, side_out):
        relayout(i_ref, o_ref, jnp.minimum(step, n_chunks - 1))


def _dot_nt(a, b_t):
    return lax.dot_general(a, b_t, (((1,), (1,)), ((), ())), preferred_element_type=F32)


def _cast_rows(i_ref, o_ref, chunk):
    o_ref[...] = i_ref[...].astype(BF16)


def _cast_rows_zero_tail(valid_rows):
    def relayout(i_ref, o_ref, chunk):
        row = chunk * i_ref.shape[0] + lax.broadcasted_iota(jnp.int32, i_ref.shape, 0)
        o_ref[...] = jnp.where(row < valid_rows, i_ref[...], 0.0).astype(BF16)
    return relayout


def _cast_tile_pairs(d_ff, tn):
    def relayout(i_ref, o_ref, chunk):
        for t in range(o_ref.shape[0]):
            lo = t * tn
            width = min(tn, d_ff - lo)
            o_ref[t, :, 0:width] = i_ref[:, lo:lo + width].astype(BF16)
            o_ref[t, :, tn:tn + width] = i_ref[:, d_ff + lo:d_ff + lo + width].astype(BF16)
            if width < tn:
                pad = jnp.zeros((o_ref.shape[1], tn - width), BF16)
                o_ref[t, :, width:tn] = pad
                o_ref[t, :, tn + width:2 * tn] = pad
    return relayout


def _inproj(x2, g, w_main_t, b_main, w_gate_t, b_gate, side, *, tm=1024, tn=1024):
    M, D = x2.shape
    N = w_main_t.shape[0]
    assert M % tm == 0 and N % tn == 0 and w_gate_t.shape == (2 * LANES, D)
    nj = N // tn
    n_steps = (M // tm) * nj

    def chunk_map(n_chunks, axis, rank):
        def index_map(i, j):
            idx = [0] * rank
            idx[axis] = jnp.minimum(i * nj + j, n_chunks - 1)
            return tuple(idx)
        return index_map

    side_in_specs, side_out_specs, side_out_shapes, side_meta = [], [], [], []
    for arr, relayout, in_block, out_block, out_shape, n_in, n_out in side:
        assert n_out <= n_steps and n_in <= n_out
        side_in_specs.append(pl.BlockSpec(in_block, chunk_map(n_in, 0, 2)))
        side_out_specs.append(pl.BlockSpec(out_block, chunk_map(n_out, len(out_block) - 2,
                                                                len(out_block))))
        side_out_shapes.append(jax.ShapeDtypeStruct(out_shape, BF16))
        side_meta.append((relayout, n_out))

    outs = pl.pallas_call(
        functools.partial(_inproj_kernel, row_chunk=128, side=tuple(side_meta)),
        grid=(M // tm, nj),
        in_specs=[
            pl.BlockSpec((tm, D), lambda i, j: (i, 0)),
            pl.BlockSpec((1, D), lambda i, j: (0, 0)),
            pl.BlockSpec((tn, D), lambda i, j: (j, 0)),
            pl.BlockSpec((1, tn), lambda i, j: (0, j)),
            pl.BlockSpec((2 * LANES, D), lambda i, j: (0, 0)),
            pl.BlockSpec((1, 2 * LANES), lambda i, j: (0, 0)),
        ] + side_in_specs,
        out_specs=[
            pl.BlockSpec((tm, tn), lambda i, j: (i, j)),
            pl.BlockSpec((tm, 2 * LANES), lambda i, j: (i, 0)),
        ] + side_out_specs,
        out_shape=[
            jax.ShapeDtypeStruct((M, N), BF16),
            jax.ShapeDtypeStruct((M, 2 * LANES), F32),
        ] + side_out_shapes,
        scratch_shapes=[pltpu.VMEM((tm, D), BF16)],
        compiler_params=_params(2),
        name="inproj",
    )(x2, g, w_main_t, b_main, w_gate_t, b_gate, *[s[0] for s in side])
    return outs[0], outs[1], outs[2:]


def _split3(a):
    hi = a.astype(BF16)
    r1 = a - hi.astype(F32)
    mid = r1.astype(BF16)
    lo = (r1 - mid.astype(F32)).astype(BF16)
    return hi, mid, lo


def _mlstm_kernel(p_ref, g_ref, hg_ref, out_ref, st_ref, m_ref):
    c = pl.program_id(0)

    @pl.when(c == 0)
    def _():
        st_ref[...] = jnp.zeros_like(st_ref)
        m_ref[...] = jnp.zeros_like(m_ref)

    for seq in range(p_ref.shape[0]):
        _mlstm_chunk(p_ref.at[seq], g_ref.at[seq], hg_ref, out_ref.at[seq], st_ref.at[seq],
                     m_ref.at[seq])


def _mlstm_chunk(p_ref, g_ref, hg_ref, out_ref, st_ref, m_ref):
    L = p_ref.shape[0]

    logi = g_ref[:, :LANES]
    f_pre = g_ref[:, LANES:]
    logf = jnp.minimum(f_pre, 0.0) - jnp.log1p(jnp.exp(-jnp.abs(f_pre)))

    row = lax.broadcasted_iota(jnp.int32, (L, L), 0)
    col = lax.broadcasted_iota(jnp.int32, (L, L), 1)
    causal = row >= col
    tri = causal.astype(BF16)
    parts = _split3(logf)
    b = (jnp.dot(tri, parts[2], preferred_element_type=F32)
         + jnp.dot(tri, parts[1], preferred_element_type=F32)
         + jnp.dot(tri, parts[0], preferred_element_type=F32))
    g = logi - b

    trow = lax.broadcasted_iota(jnp.int32, g.shape, 0)
    run = g
    shift = 1
    while shift < L:
        run = jnp.maximum(run, jnp.where(trow >= shift, pltpu.roll(run, shift, 0), -jnp.inf))
        shift *= 2
    m_prev_row = m_ref[0:1, :]
    big_m = jnp.maximum(run, m_prev_row)
    m_last = big_m[L - 1:L, :]
    m_ref[...] = jnp.broadcast_to(b[L - 1:L, :] + m_last, m_ref.shape)
    e_col = jnp.exp(-(b + big_m))
    g_t = g.T

    ones_blk = jnp.ones((L, LANES), BF16)
    for h in range(M_HEADS):
        q = p_ref[:, h * M_DQK:(h + 1) * M_DQK]
        k = p_ref[:, M_QK_WIDTH + h * M_DQK:M_QK_WIDTH + (h + 1) * M_DQK]
        v = p_ref[:, 2 * M_QK_WIDTH + h * M_DV:2 * M_QK_WIDTH + (h + 1) * M_DV]
        o_pre = p_ref[:, 2 * M_QK_WIDTH + M_WIDTH + h * M_DV:
                      2 * M_QK_WIDTH + M_WIDTH + (h + 1) * M_DV].astype(F32)
        v_ext = jnp.concatenate([v, ones_blk], axis=1)

        g_row = g_t[h:h + 1, :]
        m_rep = jnp.broadcast_to(big_m[:, h:h + 1], (L, LANES))
        m_prev = m_prev_row[:, h:h + 1]
        m_end = m_last[:, h:h + 1]

        arg = (g_row + LOG_QK_SCALE) - jnp.concatenate([m_rep] * (L // LANES), axis=1)
        qk = lax.dot_general(q, k, (((1,), (1,)), ((), ())), preferred_element_type=F32)
        S = (qk * jnp.exp(jnp.where(causal, arg, -jnp.inf))).astype(BF16)
        intra = jnp.dot(S, v_ext, preferred_element_type=F32)
        Ct = st_ref[h]
        inter = jnp.dot(q, Ct.astype(BF16), preferred_element_type=F32)
        isc = jnp.exp((m_prev + LOG_QK_SCALE) - m_rep)
        tot = intra + jnp.concatenate([isc] * (intra.shape[1] // LANES), axis=1) * inter
        num = tot[:, :M_DV]

        den = tot[:, M_DV:M_DV + 1]
        f = 1.0 / jnp.maximum(jnp.abs(den), e_col[:, h:h + 1])
        ms = jnp.mean(num * num, axis=-1, keepdims=True)
        fac = f * lax.rsqrt(ms * (f * f) + EPS)
        hh = num * fac * hg_ref[:, h * M_DV:(h + 1) * M_DV]
        out_ref[:, h * M_DV:(h + 1) * M_DV] = (hh * _sigmoid(o_pre)).astype(out_ref.dtype)

        kw_t = (k.T.astype(F32) * jnp.exp(g_row - m_end)).astype(BF16)
        upd = jnp.dot(kw_t, v_ext, preferred_element_type=F32)
        st_ref[h] = jnp.exp(m_prev - m_end) * Ct + upd


def _mlstm(proj, gates, head_g, *, batch, seq, chunk=256):
    M = proj.shape[0]
    nc = seq // chunk
    assert seq % chunk == 0 and M == batch * seq and chunk % LANES == 0
    width = 2 * M_QK_WIDTH + 2 * M_WIDTH
    hm = pl.pallas_call(
        _mlstm_kernel,
        grid=(nc,),
        in_specs=[
            pl.BlockSpec((batch, chunk, width), lambda c: (0, c, 0)),
            pl.BlockSpec((batch, chunk, 2 * LANES), lambda c: (0, c, 0)),
            pl.BlockSpec((1, M_WIDTH), lambda c: (0, 0)),
        ],
        out_specs=pl.BlockSpec((batch, chunk, M_WIDTH), lambda c: (0, c, 0)),
        out_shape=jax.ShapeDtypeStruct((batch, seq, M_WIDTH), BF16),
        scratch_shapes=[
            pltpu.VMEM((batch, M_HEADS, M_DQK, M_DV + LANES), F32),
            pltpu.VMEM((batch, 8, LANES), F32),
        ],
        compiler_params=_params(1),
        name="mlstm",
    )(proj.reshape(batch, seq, -1), gates.reshape(batch, seq, -1), head_g)
    return hm.reshape(M, M_WIDTH)


HALO = 8


def _causal_conv3(cur, prev, w_ref):
    n = prev.shape[0]
    ext = jnp.concatenate([prev, cur], axis=0)
    r1 = pltpu.roll(ext, 1, 0)[n:, :]
    r2 = pltpu.roll(ext, 2, 0)[n:, :]
    return r2 * w_ref[0:1, :] + r1 * w_ref[1:2, :] + cur * w_ref[2:3, :]


def _mix_kernel(hm_ref, cb_ref, halo_ref, gm_ref, gc_ref, x_ref, cw_ref, wbm_ref, wbc_ref,
                wout_ref, g2_ref, h1_ref, hn2_ref, merged_ref, *, blocks_per_seq, col_chunk):
    i = pl.program_id(0)
    tm = hm_ref.shape[0]
    D = x_ref.shape[1]

    cb = cb_ref[:, 0:C_WIDTH].astype(F32)
    pp = cb_ref[:, C_WIDTH:2 * C_WIDTH].astype(F32) * cb_ref[:, 2 * C_WIDTH:3 * C_WIDTH].astype(F32)
    hp = halo_ref[:, 0:C_WIDTH].astype(F32) * halo_ref[:, C_WIDTH:2 * C_WIDTH].astype(F32)
    hp = jnp.where(i % blocks_per_seq != 0, hp, 0.0)
    z = (cb * _causal_conv3(pp, hp[halo_ref.shape[0] - HALO:, :], cw_ref)).astype(BF16)
    hm = hm_ref[...]

    for n in range(D // col_chunk):
        cols = slice(n * col_chunk, (n + 1) * col_chunk)
        y_m = jnp.dot(hm, wbm_ref[:, cols], preferred_element_type=F32)
        y_c = jnp.dot(z, wbc_ref[:, cols], preferred_element_type=F32)
        merged_ref[:, cols] = (_sigmoid(gm_ref[:, cols].astype(F32)) * y_m
                               + _sigmoid(gc_ref[:, cols].astype(F32)) * y_c).astype(BF16)

    ssq = jnp.zeros((tm, 1), F32)
    for n in range(D // col_chunk):
        cols = slice(n * col_chunk, (n + 1) * col_chunk)
        h1 = x_ref[:, cols] + jnp.dot(merged_ref[...], wout_ref[:, cols],
                                      preferred_element_type=F32)
        h1_ref[:, cols] = h1
        ssq = ssq + jnp.sum(h1 * h1, axis=-1, keepdims=True)
    rs = lax.rsqrt(ssq / D + EPS)
    for n in range(D // col_chunk):
        cols = slice(n * col_chunk, (n + 1) * col_chunk)
        hn2_ref[:, cols] = (h1_ref[:, cols] * rs * g2_ref[:, cols]).astype(hn2_ref.dtype)


def _mix(hm, proj, x2, conv_w, w_bm, w_bc, w_out, g2, *, seq, tm=256, halo=16):
    M, D = x2.shape
    assert M % tm == 0 and seq % tm == 0 and tm % halo == 0
    cb_blk = MAIN_SPLIT // (3 * C_WIDTH)
    assert cb_blk * 3 * C_WIDTH == MAIN_SPLIT
    cc_off = MAIN_SPLIT + C_WIDTH
    gm_off = MAIN_SPLIT + 3 * C_WIDTH
    assert cc_off % (2 * C_WIDTH) == 0 and gm_off % D == 0
    const = dict(pipeline_mode=pl.Buffered(1))
    return pl.pallas_call(
        functools.partial(_mix_kernel, blocks_per_seq=seq // tm, col_chunk=512),
        grid=(M // tm,),
        in_specs=[
            pl.BlockSpec((tm, M_WIDTH), lambda i: (i, 0)),
            pl.BlockSpec((tm, 3 * C_WIDTH), lambda i: (i, cb_blk)),
            pl.BlockSpec((halo, 2 * C_WIDTH),
                         lambda i: (jnp.maximum(i * (tm // halo) - 1, 0), cc_off // (2 * C_WIDTH))),
            pl.BlockSpec((tm, D), lambda i: (i, gm_off // D)),
            pl.BlockSpec((tm, D), lambda i: (i, gm_off // D + 1)),
            pl.BlockSpec((tm, D), lambda i: (i, 0)),
            pl.BlockSpec(conv_w.shape, lambda i: (0, 0)),
            pl.BlockSpec(w_bm.shape, lambda i: (0, 0), **const),
            pl.BlockSpec(w_bc.shape, lambda i: (0, 0), **const),
            pl.BlockSpec(w_out.shape, lambda i: (0, 0), **const),
            pl.BlockSpec((1, D), lambda i: (0, 0)),
        ],
        out_specs=[
            pl.BlockSpec((tm, D), lambda i: (i, 0)),
            pl.BlockSpec((tm, D), lambda i: (i, 0)),
        ],
        out_shape=[
            jax.ShapeDtypeStruct((M, D), F32),
            jax.ShapeDtypeStruct((M, D), BF16),
        ],
        scratch_shapes=[pltpu.VMEM((tm, D), BF16)],
        compiler_params=_params(1),
        name="mix",
    )(hm, proj, proj, proj, proj, x2, conv_w, w_bm, w_bc, w_out, g2)


def _ffn_kernel(hn_ref, h1_ref, wu_ref, cw_ref, cb_ref, wd_ref, g3_ref, out_ref,
                u_ref, a_ref, carry_ref, *, nj, blocks_per_seq, norm_chunk=64):
    s = pl.program_id(0)
    tm, tn = hn_ref.shape[0], wd_ref.shape[0]
    tile_b = jnp.maximum(s - 1, 0)
    ib, jb = tile_b // nj, tile_b % nj
    jc = (s + nj - 2) % nj

    @pl.when(s == 0)
    def _():
        u_ref[...] = jnp.zeros_like(u_ref)
        a_ref[...] = jnp.zeros_like(a_ref)
        carry_ref[...] = jnp.zeros_like(carry_ref)

    @pl.when((jc == 0) | (s == 0))
    def _():
        out_ref[...] = h1_ref[...]

    out_ref[...] += jnp.dot(a_ref[...], wd_ref[...], preferred_element_type=F32)

    prev = jnp.where(ib % blocks_per_seq == 0, 0.0, carry_ref[jb])
    carry_ref[jb] = u_ref[tm - HALO:tm, :]
    y = _causal_conv3(u_ref[...], prev, cw_ref) + cb_ref[...]
    gate, val = y[:, :tn], y[:, tn:]
    a_ref[...] = (gate * _sigmoid(gate) * val).astype(BF16)

    u_ref[...] = jnp.dot(hn_ref[...], wu_ref[...], preferred_element_type=F32)

    @pl.when((jc == nj - 1) & (s >= 2))
    def _():
        def body(r, carry):
            rows = pl.ds(pl.multiple_of(r * norm_chunk, norm_chunk), norm_chunk)
            h2 = out_ref[rows, :]
            rs = lax.rsqrt(jnp.mean(h2 * h2, axis=-1, keepdims=True) + EPS)
            out_ref[rows, :] = h2 * rs * g3_ref[...]
            return carry
        lax.fori_loop(0, tm // norm_chunk, body, 0, unroll=True)


def _ffn(hn2, h1, w_up_t, cw_t, cb_t, w_down_p, g3, *, seq, tm=512):
    M, D = h1.shape
    nj, _, tn2 = w_up_t.shape
    tn = tn2 // 2
    assert M % tm == 0 and seq % tm == 0 and w_down_p.shape[0] == nj * tn
    ni = M // tm
    n_tiles = ni * nj
    row_a = lambda s: (jnp.minimum(s // nj, ni - 1), 0)
    row_c = lambda s: (jnp.maximum(s - 2, 0) // nj, 0)
    tile_a = lambda s: (s % nj, 0, 0)
    tile_b = lambda s: (jnp.maximum(s - 1, 0) % nj, 0, 0)
    return pl.pallas_call(
        functools.partial(_ffn_kernel, nj=nj, blocks_per_seq=seq // tm),
        grid=(n_tiles + 2,),
        in_specs=[
            pl.BlockSpec((tm, D), row_a),
            pl.BlockSpec((tm, D), row_c),
            pl.BlockSpec((None, D, tn2), tile_a),
            pl.BlockSpec((None, 3, tn2), tile_b),
            pl.BlockSpec((None, 1, tn2), tile_b),
            pl.BlockSpec((tn, D), lambda s: ((s + nj - 2) % nj, 0)),
            pl.BlockSpec((1, D), lambda s: (0, 0)),
        ],
        out_specs=pl.BlockSpec((tm, D), row_c),
        out_shape=jax.ShapeDtypeStruct((M, D), F32),
        scratch_shapes=[
            pltpu.VMEM((tm, tn2), F32),
            pltpu.VMEM((tm, tn), BF16),
            pltpu.VMEM((nj, HALO, tn2), F32),
        ],
        compiler_params=_params(1),
        name="ffn",
    )(hn2, h1, w_up_t, cw_t, cb_t, w_down_p, g3)


def _win_prep_kernel(w_ref, tail_ref, gate_ref, o_ref, og_ref, *, first_shifted):
    i = pl.program_id(0)
    m = w_ref[...]
    shifted = jnp.concatenate([m[GATE_COLS:, :], tail_ref[...]], axis=0)
    o_ref[...] = jnp.where(i >= first_shifted, shifted, m).astype(BF16)

    gr = gate_ref[...]
    lead = lax.broadcasted_iota(jnp.int32, gr.shape, 0) < M_HEADS
    zeros = jnp.zeros((LANES - GATE_COLS, gr.shape[1]), F32)
    og_ref[...] = jnp.concatenate(
        [jnp.where(lead, gr, 0.0), zeros,
         jnp.where(lead, pltpu.roll(gr, M_HEADS, 0), 0.0), zeros], axis=0).astype(BF16)


def _win_prep(w_in_t, *, rows=512):
    n, D = w_in_t.shape
    n_main = n - GATE_COLS
    assert n_main % rows == 0 and MAIN_SPLIT % rows == 0 and rows % GATE_COLS == 0
    return pl.pallas_call(
        functools.partial(_win_prep_kernel, first_shifted=MAIN_SPLIT // rows),
        grid=(n_main // rows,),
        in_specs=[
            pl.BlockSpec((rows, D), lambda i: (i, 0)),
            pl.BlockSpec((GATE_COLS, D), lambda i: ((i + 1) * (rows // GATE_COLS), 0)),
            pl.BlockSpec((GATE_COLS, D), lambda i: (MAIN_SPLIT // GATE_COLS, 0)),
        ],
        out_specs=[
            pl.BlockSpec((rows, D), lambda i: (i, 0)),
            pl.BlockSpec((2 * LANES, D), lambda i: (0, 0)),
        ],
        out_shape=[
            jax.ShapeDtypeStruct((n_main, D), BF16),
            jax.ShapeDtypeStruct((2 * LANES, D), BF16),
        ],
        compiler_params=_params(1),
        name="win_prep",
    )(w_in_t, w_in_t, w_in_t)


def _side_casts(w_up, w_down, w_out, w_branch_m, w_branch_c, *, d_ff, tn, chunks=64, down_rows=LANES):
    D = w_up.shape[0]
    nj = pl.cdiv(d_ff, tn)
    ffp = nj * tn
    assert d_ff % LANES == 0 and tn % LANES == 0 and ffp % down_rows == 0

    def rows_cast(w):
        r = w.shape[0] // chunks
        assert r * chunks == w.shape[0] and r % 16 == 0
        return (w, _cast_rows, (r, w.shape[1]), (r, w.shape[1]), w.shape, chunks, chunks)

    up_rows = D // chunks
    assert up_rows * chunks == D and up_rows % 16 == 0
    return [
        (w_up, _cast_tile_pairs(d_ff, tn), (up_rows, 2 * d_ff), (nj, up_rows, 2 * tn),
         (nj, D, 2 * tn), chunks, chunks),
        (w_down, _cast_rows_zero_tail(d_ff), (down_rows, D), (down_rows, D), (ffp, D),
         pl.cdiv(d_ff, down_rows), ffp // down_rows),
        rows_cast(w_out), rows_cast(w_branch_m), rows_cast(w_branch_c),
    ]


def _tile_pairs(a, d_ff, tn):
    nj = pl.cdiv(d_ff, tn)
    pad = lambda h: jnp.pad(h, ((0, 0), (0, nj * tn - d_ff))).reshape(a.shape[0], nj, tn)
    both = jnp.concatenate([pad(a[:, :d_ff]), pad(a[:, d_ff:])], axis=2)
    return jnp.transpose(both, (1, 0, 2))


def kernel(x, norm_mix_g, w_in, b_in, mlstm_head_g, w_branch_m, conv_w, w_branch_c, w_out,
           norm_ffn_g, w_up, ffn_conv_w, ffn_conv_b, w_down, norm_out_g):
    B, T, D = x.shape
    M = B * T
    d_ff = w_down.shape[0]
    ff_tile = 512
    ffp = pl.cdiv(d_ff, ff_tile) * ff_tile
    x2 = x.reshape(M, D)

    g0, g1 = MAIN_SPLIT, MAIN_SPLIT + GATE_COLS
    gi = g0 + M_HEADS
    w_in_t = w_in.T
    w_main_t, w_gate_t = _win_prep(w_in_t)
    lead = lambda a: jnp.pad(a, ((0, LANES - M_HEADS), (0, 0)))
    b_main = jnp.concatenate([b_in[:g0], b_in[g1:]])[None, :]
    b_gate = jnp.concatenate([lead(b_in[g0:gi, None]), lead(b_in[gi:g1, None])], axis=0).T
    cw_t = _tile_pairs(ffn_conv_w, d_ff, ff_tile)
    cb_t = _tile_pairs(ffn_conv_b[None, :], d_ff, ff_tile)

    side = _side_casts(w_up, w_down, w_out, w_branch_m, w_branch_c, d_ff=d_ff, tn=ff_tile)
    proj, gates, (w_up_t, w_down_p, w_out_b, w_bm_b, w_bc_b) = _inproj(
        x2, norm_mix_g[None, :], w_main_t, b_main, w_gate_t, b_gate, side)
    hm = _mlstm(proj, gates, mlstm_head_g[None, :], batch=B, seq=T)
    h1, hn2 = _mix(hm, proj, x2, conv_w, w_bm_b, w_bc_b, w_out_b, norm_ffn_g[None, :], seq=T)
    out = _ffn(hn2, h1, w_up_t, cw_t, cb_t, w_down_p, norm_out_g[None, :], seq=T)
    return out.reshape(B, T, D)
```

```python
import functools
import math

import jax
import jax.numpy as jnp
from jax import lax
from jax.experimental import pallas as pl
from jax.experimental.pallas import tpu as pltpu

F32 = jnp.float32
BF16 = jnp.bfloat16

EPS = 1e-6
M_HEADS = 4
M_DV = 256
M_DQK = 128
M_WIDTH = M_HEADS * M_DV
M_QK_WIDTH = M_HEADS * M_DQK
C_WIDTH = 1024
LOG_QK_SCALE = -0.5 * math.log(M_DQK)

LANES = 128
GATE_COLS = 2 * M_HEADS
MAIN_SPLIT = 2 * M_QK_WIDTH + 2 * M_WIDTH

VMEM_LIMIT = 56 * 1024 * 1024


def _params(n_axes):
    return pltpu.CompilerParams(dimension_semantics=("arbitrary",) * n_axes,
                                vmem_limit_bytes=VMEM_LIMIT)


def _sigmoid(v):
    return 1.0 / (1.0 + jnp.exp(-v))


def _inproj_kernel(x_ref, g_ref, w_ref, b_ref, wg_ref, bg_ref, proj_ref, gates_ref, hn_ref,
                   *, row_chunk):
    j = pl.program_id(1)
    tm = x_ref.shape[0]

    @pl.when(j == 0)
    def _():
        def body(r, carry):
            rows = pl.ds(pl.multiple_of(r * row_chunk, row_chunk), row_chunk)
            xb = x_ref[rows, :]
            ms = jnp.mean(xb * xb, axis=-1, keepdims=True)
            hn_ref[rows, :] = (xb * lax.rsqrt(ms + EPS) * g_ref[...]).astype(BF16)
            return carry
        lax.fori_loop(0, tm // row_chunk, body, 0, unroll=True)
        gates_ref[...] = _dot_nt(hn_ref[...], wg_ref[...]) + bg_ref[...]

    proj_ref[...] = (_dot_nt(hn_ref[...], w_ref[...]) + b_ref[...]).astype(proj_ref.dtype)


def _dot_nt(a, b_t):
    return lax.dot_general(a, b_t, (((1,), (1,)), ((), ())), preferred_element_type=F32)


def _cast_rows(i_ref, o_ref, chunk):
    o_ref[...] = i_ref[...].astype(BF16)


def _cast_rows_zero_tail(valid_rows):
    def relayout(i_ref, o_ref, chunk):
        row = chunk * i_ref.shape[0] + lax.broadcasted_iota(jnp.int32, i_ref.shape, 0)
        o_ref[...] = jnp.where(row < valid_rows, i_ref[...], 0.0).astype(BF16)
    return relayout


def _cast_tile_pairs(d_ff, tn):
    def relayout(i_ref, o_ref, chunk):
        for t in range(o_ref.shape[0]):
            lo = t * tn
            width = min(tn, d_ff - lo)
            o_ref[t, :, 0:width] = i_ref[:, lo:lo + width].astype(BF16)
            o_ref[t, :, tn:tn + width] = i_ref[:, d_ff + lo:d_ff + lo + width].astype(BF16)
            if width < tn:
                pad = jnp.zeros((o_ref.shape[1], tn - width), BF16)
                o_ref[t, :, width:tn] = pad
                o_ref[t, :, tn + width:2 * tn] = pad
    return relayout


def _inproj(x2, g, w_main_t, b_main, w_gate_t, b_gate, *, tm=1024, tn=2048):
    M, D = x2.shape
    N = w_main_t.shape[0]
    assert M % tm == 0 and N % tn == 0 and w_gate_t.shape == (2 * LANES, D)
    return pl.pallas_call(
        functools.partial(_inproj_kernel, row_chunk=128),
        grid=(M // tm, N // tn),
        in_specs=[
            pl.BlockSpec((tm, D), lambda i, j: (i, 0)),
            pl.BlockSpec((1, D), lambda i, j: (0, 0)),
            pl.BlockSpec((tn, D), lambda i, j: (j, 0)),
            pl.BlockSpec((1, tn), lambda i, j: (0, j)),
            pl.BlockSpec((2 * LANES, D), lambda i, j: (0, 0)),
            pl.BlockSpec((1, 2 * LANES), lambda i, j: (0, 0)),
        ],
        out_specs=[
            pl.BlockSpec((tm, tn), lambda i, j: (i, j)),
            pl.BlockSpec((tm, 2 * LANES), lambda i, j: (i, 0)),
        ],
        out_shape=[
            jax.ShapeDtypeStruct((M, N), BF16),
            jax.ShapeDtypeStruct((M, 2 * LANES), F32),
        ],
        scratch_shapes=[pltpu.VMEM((tm, D), BF16)],
        compiler_params=_params(2),
        name="inproj",
    )(x2, g, w_main_t, b_main, w_gate_t, b_gate)


def _split3(a):
    hi = a.astype(BF16)
    r1 = a - hi.astype(F32)
    mid = r1.astype(BF16)
    lo = (r1 - mid.astype(F32)).astype(BF16)
    return hi, mid, lo


def _mlstm_kernel(p_ref, g_ref, hg_ref, *rest, side):
    n_side = len(side)
    side_in = rest[:n_side]
    out_ref = rest[n_side]
    side_out = rest[n_side + 1:2 * n_side + 1]
    st_ref, m_ref = rest[2 * n_side + 1:]
    c = pl.program_id(0)

    @pl.when(c == 0)
    def _():
        st_ref[...] = jnp.zeros_like(st_ref)
        m_ref[...] = jnp.zeros_like(m_ref)

    for seq in range(p_ref.shape[0]):
        _mlstm_chunk(p_ref.at[seq], g_ref.at[seq], hg_ref, out_ref.at[seq], st_ref.at[seq],
                     m_ref.at[seq])
    for relayout, i_ref, o_ref in zip(side, side_in, side_out):
        relayout(i_ref, o_ref, c)


def _mlstm_chunk(p_ref, g_ref, hg_ref, out_ref, st_ref, m_ref):
    L = p_ref.shape[0]

    logi = g_ref[:, :LANES]
    f_pre = g_ref[:, LANES:]
    logf = jnp.minimum(f_pre, 0.0) - jnp.log1p(jnp.exp(-jnp.abs(f_pre)))

    row = lax.broadcasted_iota(jnp.int32, (L, L), 0)
    col = lax.broadcasted_iota(jnp.int32, (L, L), 1)
    causal = row >= col
    tri = causal.astype(BF16)
    parts = _split3(logf)
    b = (jnp.dot(tri, parts[2], preferred_element_type=F32)
         + jnp.dot(tri, parts[1], preferred_element_type=F32)
         + jnp.dot(tri, parts[0], preferred_element_type=F32))
    g = logi - b

    trow = lax.broadcasted_iota(jnp.int32, g.shape, 0)
    run = g
    shift = 1
    while shift < L:
        run = jnp.maximum(run, jnp.where(trow >= shift, pltpu.roll(run, shift, 0), -jnp.inf))
        shift *= 2
    m_prev_row = m_ref[0:1, :]
    big_m = jnp.maximum(run, m_prev_row)
    m_last = big_m[L - 1:L, :]
    m_ref[...] = jnp.broadcast_to(b[L - 1:L, :] + m_last, m_ref.shape)
    e_col = jnp.exp(-(b + big_m))
    g_t = g.T

    ones_blk = jnp.ones((L, LANES), BF16)
    for h in range(M_HEADS):
        q = p_ref[:, h * M_DQK:(h + 1) * M_DQK]
        k = p_ref[:, M_QK_WIDTH + h * M_DQK:M_QK_WIDTH + (h + 1) * M_DQK]
        v = p_ref[:, 2 * M_QK_WIDTH + h * M_DV:2 * M_QK_WIDTH + (h + 1) * M_DV]
        o_pre = p_ref[:, 2 * M_QK_WIDTH + M_WIDTH + h * M_DV:
                      2 * M_QK_WIDTH + M_WIDTH + (h + 1) * M_DV].astype(F32)
        v_ext = jnp.concatenate([v, ones_blk], axis=1)

        g_row = g_t[h:h + 1, :]
        m_rep = jnp.broadcast_to(big_m[:, h:h + 1], (L, LANES))
        m_prev = m_prev_row[:, h:h + 1]
        m_end = m_last[:, h:h + 1]

        arg = (g_row + LOG_QK_SCALE) - jnp.concatenate([m_rep] * (L // LANES), axis=1)
        qk = lax.dot_general(q, k, (((1,), (1,)), ((), ())), preferred_element_type=F32)
        S = (qk * jnp.exp(jnp.where(causal, arg, -jnp.inf))).astype(BF16)
        intra = jnp.dot(S, v_ext, preferred_element_type=F32)
        Ct = st_ref[h]
        inter = jnp.dot(q, Ct.astype(BF16), preferred_element_type=F32)
        isc = jnp.exp((m_prev + LOG_QK_SCALE) - m_rep)
        tot = intra + jnp.concatenate([isc] * (intra.shape[1] // LANES), axis=1) * inter
        num = tot[:, :M_DV]

        den = tot[:, M_DV:M_DV + 1]
        f = 1.0 / jnp.maximum(jnp.abs(den), e_col[:, h:h + 1])
        ms = jnp.mean(num * num, axis=-1, keepdims=True)
        fac = f * lax.rsqrt(ms * (f * f) + EPS)
        hh = num * fac * hg_ref[:, h * M_DV:(h + 1) * M_DV]
        out_ref[:, h * M_DV:(h + 1) * M_DV] = (hh * _sigmoid(o_pre)).astype(out_ref.dtype)

        kw_t = (k.T.astype(F32) * jnp.exp(g_row - m_end)).astype(BF16)
        upd = jnp.dot(kw_t, v_ext, preferred_element_type=F32)
        st_ref[h] = jnp.exp(m_prev - m_end) * Ct + upd


def _mlstm(proj, gates, head_g, side, *, batch, seq, chunk=256):
    M = proj.shape[0]
    nc = seq // chunk
    assert seq % chunk == 0 and M == batch * seq and chunk % LANES == 0
    width = 2 * M_QK_WIDTH + 2 * M_WIDTH

    def chunk_map(axis, rank):
        return lambda c: tuple(c if a == axis else 0 for a in range(rank))

    outs = pl.pallas_call(
        functools.partial(_mlstm_kernel, side=tuple(sd[1] for sd in side)),
        grid=(nc,),
        in_specs=[
            pl.BlockSpec((batch, chunk, width), lambda c: (0, c, 0)),
            pl.BlockSpec((batch, chunk, 2 * LANES), lambda c: (0, c, 0)),
            pl.BlockSpec((1, M_WIDTH), lambda c: (0, 0)),
        ] + [pl.BlockSpec(sd[2], chunk_map(0, 2)) for sd in side],
        out_specs=[pl.BlockSpec((batch, chunk, M_WIDTH), lambda c: (0, c, 0))]
        + [pl.BlockSpec(sd[3], chunk_map(len(sd[3]) - 2, len(sd[3]))) for sd in side],
        out_shape=[jax.ShapeDtypeStruct((batch, seq, M_WIDTH), BF16)]
        + [jax.ShapeDtypeStruct(sd[4], BF16) for sd in side],
        scratch_shapes=[
            pltpu.VMEM((batch, M_HEADS, M_DQK, M_DV + LANES), F32),
            pltpu.VMEM((batch, 8, LANES), F32),
        ],
        compiler_params=_params(1),
        name="mlstm",
    )(proj.reshape(batch, seq, -1), gates.reshape(batch, seq, -1), head_g, *[sd[0] for sd in side])
    return outs[0].reshape(M, M_WIDTH), outs[1:]


HALO = 8


def _causal_conv3(cur, prev, w_ref):
    n = prev.shape[0]
    ext = jnp.concatenate([prev, cur], axis=0)
    r1 = pltpu.roll(ext, 1, 0)[n:, :]
    r2 = pltpu.roll(ext, 2, 0)[n:, :]
    return r2 * w_ref[0:1, :] + r1 * w_ref[1:2, :] + cur * w_ref[2:3, :]


def _mix_kernel(hm_ref, cb_ref, halo_ref, gm_ref, gc_ref, x_ref, cw_ref, wbm_ref, wbc_ref,
                wout_ref, g2_ref, h1_ref, hn2_ref, merged_ref, *, blocks_per_seq, col_chunk):
    i = pl.program_id(0)
    tm = hm_ref.shape[0]
    D = x_ref.shape[1]

    cb = cb_ref[:, 0:C_WIDTH].astype(F32)
    pp = cb_ref[:, C_WIDTH:2 * C_WIDTH].astype(F32) * cb_ref[:, 2 * C_WIDTH:3 * C_WIDTH].astype(F32)
    hp = halo_ref[:, 0:C_WIDTH].astype(F32) * halo_ref[:, C_WIDTH:2 * C_WIDTH].astype(F32)
    hp = jnp.where(i % blocks_per_seq != 0, hp, 0.0)
    z = (cb * _causal_conv3(pp, hp[halo_ref.shape[0] - HALO:, :], cw_ref)).astype(BF16)
    hm = hm_ref[...]

    for n in range(D // col_chunk):
        cols = slice(n * col_chunk, (n + 1) * col_chunk)
        y_m = jnp.dot(hm, wbm_ref[:, cols], preferred_element_type=F32)
        y_c = jnp.dot(z, wbc_ref[:, cols], preferred_element_type=F32)
        merged_ref[:, cols] = (_sigmoid(gm_ref[:, cols].astype(F32)) * y_m
                               + _sigmoid(gc_ref[:, cols].astype(F32)) * y_c).astype(BF16)

    ssq = jnp.zeros((tm, 1), F32)
    for n in range(D // col_chunk):
        cols = slice(n * col_chunk, (n + 1) * col_chunk)
        h1 = x_ref[:, cols] + jnp.dot(merged_ref[...], wout_ref[:, cols],
                                      preferred_element_type=F32)
        h1_ref[:, cols] = h1
        ssq = ssq + jnp.sum(h1 * h1, axis=-1, keepdims=True)
    rs = lax.rsqrt(ssq / D + EPS)
    for n in range(D // col_chunk):
        cols = slice(n * col_chunk, (n + 1) * col_chunk)
        hn2_ref[:, cols] = (h1_ref[:, cols] * rs * g2_ref[:, cols]).astype(hn2_ref.dtype)


def _mix(hm, proj, x2, conv_w, w_bm, w_bc, w_out, g2, *, seq, tm=256, halo=16):
    M, D = x2.shape
    assert M % tm == 0 and seq % tm == 0 and tm % halo == 0
    cb_blk = MAIN_SPLIT // (3 * C_WIDTH)
    assert cb_blk * 3 * C_WIDTH == MAIN_SPLIT
    cc_off = MAIN_SPLIT + C_WIDTH
    gm_off = MAIN_SPLIT + 3 * C_WIDTH
    assert cc_off % (2 * C_WIDTH) == 0 and gm_off % D == 0
    const = dict(pipeline_mode=pl.Buffered(1))
    return pl.pallas_call(
        functools.partial(_mix_kernel, blocks_per_seq=seq // tm, col_chunk=512),
        grid=(M // tm,),
        in_specs=[
            pl.BlockSpec((tm, M_WIDTH), lambda i: (i, 0)),
            pl.BlockSpec((tm, 3 * C_WIDTH), lambda i: (i, cb_blk)),
            pl.BlockSpec((halo, 2 * C_WIDTH),
                         lambda i: (jnp.maximum(i * (tm // halo) - 1, 0), cc_off // (2 * C_WIDTH))),
            pl.BlockSpec((tm, D), lambda i: (i, gm_off // D)),
            pl.BlockSpec((tm, D), lambda i: (i, gm_off // D + 1)),
            pl.BlockSpec((tm, D), lambda i: (i, 0)),
            pl.BlockSpec(conv_w.shape, lambda i: (0, 0)),
            pl.BlockSpec(w_bm.shape, lambda i: (0, 0), **const),
            pl.BlockSpec(w_bc.shape, lambda i: (0, 0), **const),
            pl.BlockSpec(w_out.shape, lambda i: (0, 0), **const),
            pl.BlockSpec((1, D), lambda i: (0, 0)),
        ],
        out_specs=[
            pl.BlockSpec((tm, D), lambda i: (i, 0)),
            pl.BlockSpec((tm, D), lambda i: (i, 0)),
        ],
        out_shape=[
            jax.ShapeDtypeStruct((M, D), F32),
            jax.ShapeDtypeStruct((M, D), BF16),
        ],
        scratch_shapes=[pltpu.VMEM((tm, D), BF16)],
        compiler_params=_params(1),
        name="mix",
    )(hm, proj, proj, proj, proj, x2, conv_w, w_bm, w_bc, w_out, g2)


def _ffn_kernel(hn_ref, h1_ref, wu_ref, cw_ref, cb_ref, wd_ref, g3_ref, out_ref,
                u_ref, a_ref, carry_ref, *, nj, blocks_per_seq, norm_chunk=64):
    s = pl.program_id(0)
    tm, tn = hn_ref.shape[0], wd_ref.shape[0]
    tile_b = jnp.maximum(s - 1, 0)
    ib, jb = tile_b // nj, tile_b % nj
    jc = (s + nj - 2) % nj

    @pl.when(s == 0)
    def _():
        u_ref[...] = jnp.zeros_like(u_ref)
        a_ref[...] = jnp.zeros_like(a_ref)
        carry_ref[...] = jnp.zeros_like(carry_ref)

    @pl.when((jc == 0) | (s == 0))
    def _():
        out_ref[...] = h1_ref[...]

    out_ref[...] += jnp.dot(a_ref[...], wd_ref[...], preferred_element_type=F32)

    prev = jnp.where(ib % blocks_per_seq == 0, 0.0, carry_ref[jb])
    carry_ref[jb] = u_ref[tm - HALO:tm, :]
    y = _causal_conv3(u_ref[...], prev, cw_ref) + cb_ref[...]
    gate, val = y[:, :tn], y[:, tn:]
    a_ref[...] = (gate * _sigmoid(gate) * val).astype(BF16)

    u_ref[...] = jnp.dot(hn_ref[...], wu_ref[...], preferred_element_type=F32)

    @pl.when((jc == nj - 1) & (s >= 2))
    def _():
        def body(r, carry):
            rows = pl.ds(pl.multiple_of(r * norm_chunk, norm_chunk), norm_chunk)
            h2 = out_ref[rows, :]
            rs = lax.rsqrt(jnp.mean(h2 * h2, axis=-1, keepdims=True) + EPS)
            out_ref[rows, :] = h2 * rs * g3_ref[...]
            return carry
        lax.fori_loop(0, tm // norm_chunk, body, 0, unroll=True)


def _ffn(hn2, h1, w_up_t, cw_t, cb_t, w_down_p, g3, *, seq, tm=512):
    M, D = h1.shape
    nj, _, tn2 = w_up_t.shape
    tn = tn2 // 2
    assert M % tm == 0 and seq % tm == 0 and w_down_p.shape[0] == nj * tn
    ni = M // tm
    n_tiles = ni * nj
    row_a = lambda s: (jnp.minimum(s // nj, ni - 1), 0)
    row_c = lambda s: (jnp.maximum(s - 2, 0) // nj, 0)
    tile_a = lambda s: (s % nj, 0, 0)
    tile_b = lambda s: (jnp.maximum(s - 1, 0) % nj, 0, 0)
    return pl.pallas_call(
        functools.partial(_ffn_kernel, nj=nj, blocks_per_seq=seq // tm),
        grid=(n_tiles + 2,),
        in_specs=[
            pl.BlockSpec((tm, D), row_a),
            pl.BlockSpec((tm, D), row_c),
            pl.BlockSpec((None, D, tn2), tile_a),
            pl.BlockSpec((None, 3, tn2), tile_b),
            pl.BlockSpec((None, 1, tn2), tile_b),
            pl.BlockSpec((tn, D), lambda s: ((s + nj - 2) % nj, 0)),
            pl.BlockSpec((1, D), lambda s: (0, 0)),
        ],
        out_specs=pl.BlockSpec((tm, D), row_c),
        out_shape=jax.ShapeDtypeStruct((M, D), F32),
        scratch_shapes=[
            pltpu.VMEM((tm, tn2), F32),
            pltpu.VMEM((tm, tn), BF16),
            pltpu.VMEM((nj, HALO, tn2), F32),
        ],
        compiler_params=_params(1),
        name="ffn",
    )(hn2, h1, w_up_t, cw_t, cb_t, w_down_p, g3)


def _win_prep_kernel(w_ref, tail_ref, gate_ref, o_ref, og_ref, *, first_shifted):
    i = pl.program_id(0)
    m = w_ref[...]
    shifted = jnp.concatenate([m[GATE_COLS:, :], tail_ref[...]], axis=0)
    o_ref[...] = jnp.where(i >= first_shifted, shifted, m).astype(BF16)

    gr = gate_ref[...]
    lead = lax.broadcasted_iota(jnp.int32, gr.shape, 0) < M_HEADS
    zeros = jnp.zeros((LANES - GATE_COLS, gr.shape[1]), F32)
    og_ref[...] = jnp.concatenate(
        [jnp.where(lead, gr, 0.0), zeros,
         jnp.where(lead, pltpu.roll(gr, M_HEADS, 0), 0.0), zeros], axis=0).astype(BF16)


def _win_prep(w_in_t, *, rows=512):
    n, D = w_in_t.shape
    n_main = n - GATE_COLS
    assert n_main % rows == 0 and MAIN_SPLIT % rows == 0 and rows % GATE_COLS == 0
    return pl.pallas_call(
        functools.partial(_win_prep_kernel, first_shifted=MAIN_SPLIT // rows),
        grid=(n_main // rows,),
        in_specs=[
            pl.BlockSpec((rows, D), lambda i: (i, 0)),
            pl.BlockSpec((GATE_COLS, D), lambda i: ((i + 1) * (rows // GATE_COLS), 0)),
            pl.BlockSpec((GATE_COLS, D), lambda i: (MAIN_SPLIT // GATE_COLS, 0)),
        ],
        out_specs=[
            pl.BlockSpec((rows, D), lambda i: (i, 0)),
            pl.BlockSpec((2 * LANES, D), lambda i: (0, 0)),
        ],
        out_shape=[
            jax.ShapeDtypeStruct((n_main, D), BF16),
            jax.ShapeDtypeStruct((2 * LANES, D), BF16),
        ],
        compiler_params=_params(1),
        name="win_prep",
    )(w_in_t, w_in_t, w_in_t)


def _side_casts(w_up, w_down, w_out, w_branch_m, w_branch_c, *, d_ff, tn, chunks):
    D = w_up.shape[0]
    nj = pl.cdiv(d_ff, tn)
    ffp = nj * tn
    assert d_ff % LANES == 0 and tn % LANES == 0

    def rows_per_chunk(total):
        r = total // chunks
        assert r * chunks == total and r % 16 == 0
        return r

    def rows_cast(w):
        r = rows_per_chunk(w.shape[0])
        return (w, _cast_rows, (r, w.shape[1]), (r, w.shape[1]), w.shape)

    up_rows, down_rows = rows_per_chunk(D), rows_per_chunk(ffp)
    assert (chunks - 1) * down_rows < d_ff
    return [
        (w_up, _cast_tile_pairs(d_ff, tn), (up_rows, 2 * d_ff), (nj, up_rows, 2 * tn),
         (nj, D, 2 * tn)),
        (w_down, _cast_rows_zero_tail(d_ff), (down_rows, D), (down_rows, D), (ffp, D)),
        rows_cast(w_out), rows_cast(w_branch_m), rows_cast(w_branch_c),
    ]


def _tile_pairs(a, d_ff, tn):
    nj = pl.cdiv(d_ff, tn)
    pad = lambda h: jnp.pad(h, ((0, 0), (0, nj * tn - d_ff))).reshape(a.shape[0], nj, tn)
    both = jnp.concatenate([pad(a[:, :d_ff]), pad(a[:, d_ff:])], axis=2)
    return jnp.transpose(both, (1, 0, 2))


def kernel(x, norm_mix_g, w_in, b_in, mlstm_head_g, w_branch_m, conv_w, w_branch_c, w_out,
           norm_ffn_g, w_up, ffn_conv_w, ffn_conv_b, w_down, norm_out_g):
    B, T, D = x.shape
    M = B * T
    d_ff = w_down.shape[0]
    ff_tile = 512
    x2 = x.reshape(M, D)

    g0, g1 = MAIN_SPLIT, MAIN_SPLIT + GATE_COLS
    gi = g0 + M_HEADS
    w_in_t = w_in.T
    w_main_t, w_gate_t = _win_prep(w_in_t)
    lead = lambda a: jnp.pad(a, ((0, LANES - M_HEADS), (0, 0)))
    b_main = jnp.concatenate([b_in[:g0], b_in[g1:]])[None, :]
    b_gate = jnp.concatenate([lead(b_in[g0:gi, None]), lead(b_in[gi:g1, None])], axis=0).T
    cw_t = _tile_pairs(ffn_conv_w, d_ff, ff_tile)
    cb_t = _tile_pairs(ffn_conv_b[None, :], d_ff, ff_tile)

    proj, gates = _inproj(x2, norm_mix_g[None, :], w_main_t, b_main, w_gate_t, b_gate)
    chunk = 256
    side = _side_casts(w_up, w_down, w_out, w_branch_m, w_branch_c, d_ff=d_ff, tn=ff_tile,
                       chunks=T // chunk)
    hm, (w_up_t, w_down_p, w_out_b, w_bm_b, w_bc_b) = _mlstm(
        proj, gates, mlstm_head_g[None, :], side, batch=B, seq=T, chunk=chunk)
    h1, hn2 = _mix(hm, proj, x2, conv_w, w_bm_b, w_bc_b, w_out_b, norm_ffn_g[None, :], seq=T)
    out = _ffn(hn2, h1, w_up_t, cw_t, cb_t, w_down_p, norm_out_g[None, :], seq=T)
    return out.reshape(B, T, D)
```

```python
import functools
import math

import jax
import jax.numpy as jnp
from jax import lax
from jax.experimental import pallas as pl
from jax.experimental.pallas import tpu as pltpu

F32 = jnp.float32
BF16 = jnp.bfloat16

EPS = 1e-6
M_HEADS = 4
M_DV = 256
M_DQK = 128
M_WIDTH = M_HEADS * M_DV
M_QK_WIDTH = M_HEADS * M_DQK
C_WIDTH = 1024
LOG_QK_SCALE = -0.5 * math.log(M_DQK)

LANES = 128
GATE_COLS = 2 * M_HEADS
MAIN_SPLIT = 2 * M_QK_WIDTH + 2 * M_WIDTH

VMEM_LIMIT = 58 * 1024 * 1024


def _params(n_axes):
    return pltpu.CompilerParams(dimension_semantics=("arbitrary",) * n_axes,
                                vmem_limit_bytes=VMEM_LIMIT)


def _sigmoid(v):
    return 1.0 / (1.0 + jnp.exp(-v))


def _inproj_kernel(x_ref, g_ref, w_ref, b_ref, wg_ref, bg_ref, *rest, row_chunk, side):
    side_in, (proj_ref, gates_ref), side_out, (hn_ref,) = _split_side(rest, len(side), 2)
    j = pl.program_id(1)
    tm = x_ref.shape[0]

    @pl.when(j == 0)
    def _():
        def body(r, carry):
            rows = pl.ds(pl.multiple_of(r * row_chunk, row_chunk), row_chunk)
            xb = x_ref[rows, :]
            ms = jnp.mean(xb * xb, axis=-1, keepdims=True)
            hn_ref[rows, :] = (xb * lax.rsqrt(ms + EPS) * g_ref[...]).astype(BF16)
            return carry
        lax.fori_loop(0, tm // row_chunk, body, 0, unroll=True)
        gates_ref[...] = _dot_nt(hn_ref[...], wg_ref[...]) + bg_ref[...]

    proj_ref[...] = (_dot_nt(hn_ref[...], w_ref[...]) + b_ref[...]).astype(proj_ref.dtype)
    _run_side(side, side_in, side_out, pl.program_id(0) * pl.num_programs(1) + j)


def _dot_nt(a, b_t):
    return lax.dot_general(a, b_t, (((1,), (1,)), ((), ())), preferred_element_type=F32)


def _cast_rows(i_ref, o_ref, chunk):
    o_ref[...] = i_ref[...].astype(BF16)


def _cast_rows_zero_tail(valid_rows):
    def relayout(i_ref, o_ref, chunk):
        row = chunk * i_ref.shape[0] + lax.broadcasted_iota(jnp.int32, i_ref.shape, 0)
        o_ref[...] = jnp.where(row < valid_rows, i_ref[...], 0.0).astype(BF16)
    return relayout


def _cast_tile_pairs(d_ff, tn):
    def relayout(i_ref, o_ref, chunk):
        for t in range(o_ref.shape[0]):
            lo = t * tn
            width = min(tn, d_ff - lo)
            o_ref[t, :, 0:width] = i_ref[:, lo:lo + width].astype(BF16)
            o_ref[t, :, tn:tn + width] = i_ref[:, d_ff + lo:d_ff + lo + width].astype(BF16)
            if width < tn:
                pad = jnp.zeros((o_ref.shape[1], tn - width), BF16)
                o_ref[t, :, width:tn] = pad
                o_ref[t, :, tn + width:2 * tn] = pad
    return relayout


def _split_side(rest, n_side, n_out):
    return (rest[:n_side], rest[n_side:n_side + n_out],
            rest[n_side + n_out:2 * n_side + n_out], rest[2 * n_side + n_out:])


def _run_side(side, side_in, side_out, step):
    for (relayout, n_chunks), i_ref, o_ref in zip(side, side_in, side_out):
        relayout(i_ref, o_ref, jnp.minimum(step, n_chunks - 1))


def _side_specs(side, n_steps, step_of):
    def chunk_map(n_chunks, axis, rank):
        def index_map(*idx):
            chunk = jnp.minimum(step_of(*idx), n_chunks - 1)
            return tuple(chunk if a == axis else 0 for a in range(rank))
        return index_map

    in_specs, out_specs, out_shapes, meta = [], [], [], []
    for arr, relayout, in_block, out_block, out_shape, n_chunks in side:
        assert n_chunks <= n_steps
        in_specs.append(pl.BlockSpec(in_block, chunk_map(n_chunks, 0, 2)))
        out_specs.append(pl.BlockSpec(out_block, chunk_map(n_chunks, len(out_block) - 2,
                                                           len(out_block))))
        out_shapes.append(jax.ShapeDtypeStruct(out_shape, BF16))
        meta.append((relayout, n_chunks))
    return in_specs, out_specs, out_shapes, tuple(meta), [sd[0] for sd in side]


def _inproj(x2, g, w_main_t, b_main, w_gate_t, b_gate, side, *, tm=1024, tn=2048):
    M, D = x2.shape
    N = w_main_t.shape[0]
    assert M % tm == 0 and N % tn == 0 and w_gate_t.shape == (2 * LANES, D)
    nj = N // tn
    s_in, s_out, s_shapes, s_meta, s_arrays = _side_specs(
        side, (M // tm) * nj, lambda i, j: i * nj + j)
    outs = pl.pallas_call(
        functools.partial(_inproj_kernel, row_chunk=128, side=s_meta),
        grid=(M // tm, nj),
        in_specs=[
            pl.BlockSpec((tm, D), lambda i, j: (i, 0)),
            pl.BlockSpec((1, D), lambda i, j: (0, 0)),
            pl.BlockSpec((tn, D), lambda i, j: (j, 0)),
            pl.BlockSpec((1, tn), lambda i, j: (0, j)),
            pl.BlockSpec((2 * LANES, D), lambda i, j: (0, 0)),
            pl.BlockSpec((1, 2 * LANES), lambda i, j: (0, 0)),
        ] + s_in,
        out_specs=[
            pl.BlockSpec((tm, tn), lambda i, j: (i, j)),
            pl.BlockSpec((tm, 2 * LANES), lambda i, j: (i, 0)),
        ] + s_out,
        out_shape=[
            jax.ShapeDtypeStruct((M, N), BF16),
            jax.ShapeDtypeStruct((M, 2 * LANES), F32),
        ] + s_shapes,
        scratch_shapes=[pltpu.VMEM((tm, D), BF16)],
        compiler_params=_params(2),
        name="inproj",
    )(x2, g, w_main_t, b_main, w_gate_t, b_gate, *s_arrays)
    return outs[0], outs[1], outs[2:]


def _split3(a):
    hi = a.astype(BF16)
    r1 = a - hi.astype(F32)
    mid = r1.astype(BF16)
    lo = (r1 - mid.astype(F32)).astype(BF16)
    return hi, mid, lo


def _mlstm_kernel(p_ref, g_ref, hg_ref, *rest, side):
    side_in, (out_ref,), side_out, (st_ref, m_ref) = _split_side(rest, len(side), 1)
    c = pl.program_id(0)

    @pl.when(c == 0)
    def _():
        st_ref[...] = jnp.zeros_like(st_ref)
        m_ref[...] = jnp.zeros_like(m_ref)

    for seq in range(p_ref.shape[0]):
        _mlstm_chunk(p_ref.at[seq], g_ref.at[seq], hg_ref, out_ref.at[seq], st_ref.at[seq],
                     m_ref.at[seq])
    _run_side(side, side_in, side_out, c)


def _mlstm_chunk(p_ref, g_ref, hg_ref, out_ref, st_ref, m_ref):
    L = p_ref.shape[0]

    logi = g_ref[:, :LANES]
    f_pre = g_ref[:, LANES:]
    logf = jnp.minimum(f_pre, 0.0) - jnp.log1p(jnp.exp(-jnp.abs(f_pre)))

    row = lax.broadcasted_iota(jnp.int32, (L, L), 0)
    col = lax.broadcasted_iota(jnp.int32, (L, L), 1)
    causal = row >= col
    tri = causal.astype(BF16)
    parts = _split3(logf)
    b = (jnp.dot(tri, parts[2], preferred_element_type=F32)
         + jnp.dot(tri, parts[1], preferred_element_type=F32)
         + jnp.dot(tri, parts[0], preferred_element_type=F32))
    g = logi - b

    trow = lax.broadcasted_iota(jnp.int32, g.shape, 0)
    run = g
    shift = 1
    while shift < L:
        run = jnp.maximum(run, jnp.where(trow >= shift, pltpu.roll(run, shift, 0), -jnp.inf))
        shift *= 2
    m_prev_row = m_ref[0:1, :]
    big_m = jnp.maximum(run, m_prev_row)
    m_last = big_m[L - 1:L, :]
    m_ref[...] = jnp.broadcast_to(b[L - 1:L, :] + m_last, m_ref.shape)
    e_col = jnp.exp(-(b + big_m))
    g_t = g.T

    ones_blk = jnp.ones((L, LANES), BF16)
    for h in range(M_HEADS):
        q = p_ref[:, h * M_DQK:(h + 1) * M_DQK]
        k = p_ref[:, M_QK_WIDTH + h * M_DQK:M_QK_WIDTH + (h + 1) * M_DQK]
        v = p_ref[:, 2 * M_QK_WIDTH + h * M_DV:2 * M_QK_WIDTH + (h + 1) * M_DV]
        o_pre = p_ref[:, 2 * M_QK_WIDTH + M_WIDTH + h * M_DV:
                      2 * M_QK_WIDTH + M_WIDTH + (h + 1) * M_DV].astype(F32)
        v_ext = jnp.concatenate([v, ones_blk], axis=1)

        g_row = g_t[h:h + 1, :]
        m_rep = jnp.broadcast_to(big_m[:, h:h + 1], (L, LANES))
        m_prev = m_prev_row[:, h:h + 1]
        m_end = m_last[:, h:h + 1]

        arg = (g_row + LOG_QK_SCALE) - jnp.concatenate([m_rep] * (L // LANES), axis=1)
        qk = lax.dot_general(q, k, (((1,), (1,)), ((), ())), preferred_element_type=F32)
        S = (qk * jnp.exp(jnp.where(causal, arg, -jnp.inf))).astype(BF16)
        intra = jnp.dot(S, v_ext, preferred_element_type=F32)
        Ct = st_ref[h]
        inter = jnp.dot(q, Ct.astype(BF16), preferred_element_type=F32)
        isc = jnp.exp((m_prev + LOG_QK_SCALE) - m_rep)
        tot = intra + jnp.concatenate([isc] * (intra.shape[1] // LANES), axis=1) * inter
        num = tot[:, :M_DV]

        den = tot[:, M_DV:M_DV + 1]
        f = 1.0 / jnp.maximum(jnp.abs(den), e_col[:, h:h + 1])
        ms = jnp.mean(num * num, axis=-1, keepdims=True)
        fac = f * lax.rsqrt(ms * (f * f) + EPS)
        hh = num * fac * hg_ref[:, h * M_DV:(h + 1) * M_DV]
        out_ref[:, h * M_DV:(h + 1) * M_DV] = (hh * _sigmoid(o_pre)).astype(out_ref.dtype)

        kw_t = (k.T.astype(F32) * jnp.exp(g_row - m_end)).astype(BF16)
        upd = jnp.dot(kw_t, v_ext, preferred_element_type=F32)
        st_ref[h] = jnp.exp(m_prev - m_end) * Ct + upd


def _mlstm(proj, gates, head_g, side, *, batch, seq, chunk=256):
    M = proj.shape[0]
    nc = seq // chunk
    assert seq % chunk == 0 and M == batch * seq and chunk % LANES == 0
    width = 2 * M_QK_WIDTH + 2 * M_WIDTH
    s_in, s_out, s_shapes, s_meta, s_arrays = _side_specs(side, nc, lambda c: c)
    outs = pl.pallas_call(
        functools.partial(_mlstm_kernel, side=s_meta),
        grid=(nc,),
        in_specs=[
            pl.BlockSpec((batch, chunk, width), lambda c: (0, c, 0)),
            pl.BlockSpec((batch, chunk, 2 * LANES), lambda c: (0, c, 0)),
            pl.BlockSpec((1, M_WIDTH), lambda c: (0, 0)),
        ] + s_in,
        out_specs=[pl.BlockSpec((batch, chunk, M_WIDTH), lambda c: (0, c, 0))] + s_out,
        out_shape=[jax.ShapeDtypeStruct((batch, seq, M_WIDTH), BF16)] + s_shapes,
        scratch_shapes=[
            pltpu.VMEM((batch, M_HEADS, M_DQK, M_DV + LANES), F32),
            pltpu.VMEM((batch, 8, LANES), F32),
        ],
        compiler_params=_params(1),
        name="mlstm",
    )(proj.reshape(batch, seq, -1), gates.reshape(batch, seq, -1), head_g, *s_arrays)
    return outs[0].reshape(M, M_WIDTH), outs[1:]


HALO = 8


def _causal_conv3(cur, prev, w_ref):
    n = prev.shape[0]
    ext = jnp.concatenate([prev, cur], axis=0)
    r1 = pltpu.roll(ext, 1, 0)[n:, :]
    r2 = pltpu.roll(ext, 2, 0)[n:, :]
    return r2 * w_ref[0:1, :] + r1 * w_ref[1:2, :] + cur * w_ref[2:3, :]


def _mix_kernel(hm_ref, cb_ref, halo_ref, gm_ref, gc_ref, x_ref, cw_ref, wbm_ref, wbc_ref,
                wout_ref, g2_ref, *rest, blocks_per_seq, col_chunk, side):
    side_in, (h1_ref, hn2_ref), side_out, (merged_ref,) = _split_side(rest, len(side), 2)
    i = pl.program_id(0)
    tm = hm_ref.shape[0]
    D = x_ref.shape[1]

    cb = cb_ref[:, 0:C_WIDTH].astype(F32)
    pp = cb_ref[:, C_WIDTH:2 * C_WIDTH].astype(F32) * cb_ref[:, 2 * C_WIDTH:3 * C_WIDTH].astype(F32)
    hp = halo_ref[:, 0:C_WIDTH].astype(F32) * halo_ref[:, C_WIDTH:2 * C_WIDTH].astype(F32)
    hp = jnp.where(i % blocks_per_seq != 0, hp, 0.0)
    z = (cb * _causal_conv3(pp, hp[halo_ref.shape[0] - HALO:, :], cw_ref)).astype(BF16)
    hm = hm_ref[...]

    for n in range(D // col_chunk):
        cols = slice(n * col_chunk, (n + 1) * col_chunk)
        y_m = jnp.dot(hm, wbm_ref[:, cols], preferred_element_type=F32)
        y_c = jnp.dot(z, wbc_ref[:, cols], preferred_element_type=F32)
        merged_ref[:, cols] = (_sigmoid(gm_ref[:, cols].astype(F32)) * y_m
                               + _sigmoid(gc_ref[:, cols].astype(F32)) * y_c).astype(BF16)

    ssq = jnp.zeros((tm, 1), F32)
    for n in range(D // col_chunk):
        cols = slice(n * col_chunk, (n + 1) * col_chunk)
        h1 = x_ref[:, cols] + jnp.dot(merged_ref[...], wout_ref[:, cols],
                                      preferred_element_type=F32)
        h1_ref[:, cols] = h1
        ssq = ssq + jnp.sum(h1 * h1, axis=-1, keepdims=True)
    rs = lax.rsqrt(ssq / D + EPS)
    for n in range(D // col_chunk):
        cols = slice(n * col_chunk, (n + 1) * col_chunk)
        hn2_ref[:, cols] = (h1_ref[:, cols] * rs * g2_ref[:, cols]).astype(hn2_ref.dtype)
    _run_side(side, side_in, side_out, i)


def _mix(hm, proj, x2, conv_w, w_bm, w_bc, w_out, g2, side, *, seq, tm=256, halo=16):
    M, D = x2.shape
    assert M % tm == 0 and seq % tm == 0 and tm % halo == 0
    cb_blk = MAIN_SPLIT // (3 * C_WIDTH)
    assert cb_blk * 3 * C_WIDTH == MAIN_SPLIT
    cc_off = MAIN_SPLIT + C_WIDTH
    gm_off = MAIN_SPLIT + 3 * C_WIDTH
    assert cc_off % (2 * C_WIDTH) == 0 and gm_off % D == 0
    const = dict(pipeline_mode=pl.Buffered(1))
    s_in, s_out, s_shapes, s_meta, s_arrays = _side_specs(side, M // tm, lambda i: i)
    outs = pl.pallas_call(
        functools.partial(_mix_kernel, blocks_per_seq=seq // tm, col_chunk=512, side=s_meta),
        grid=(M // tm,),
        in_specs=[
            pl.BlockSpec((tm, M_WIDTH), lambda i: (i, 0)),
            pl.BlockSpec((tm, 3 * C_WIDTH), lambda i: (i, cb_blk)),
            pl.BlockSpec((halo, 2 * C_WIDTH),
                         lambda i: (jnp.maximum(i * (tm // halo) - 1, 0), cc_off // (2 * C_WIDTH))),
            pl.BlockSpec((tm, D), lambda i: (i, gm_off // D)),
            pl.BlockSpec((tm, D), lambda i: (i, gm_off // D + 1)),
            pl.BlockSpec((tm, D), lambda i: (i, 0)),
            pl.BlockSpec(conv_w.shape, lambda i: (0, 0)),
            pl.BlockSpec(w_bm.shape, lambda i: (0, 0), **const),
            pl.BlockSpec(w_bc.shape, lambda i: (0, 0), **const),
            pl.BlockSpec(w_out.shape, lambda i: (0, 0), **const),
            pl.BlockSpec((1, D), lambda i: (0, 0)),
        ] + s_in,
        out_specs=[
            pl.BlockSpec((tm, D), lambda i: (i, 0)),
            pl.BlockSpec((tm, D), lambda i: (i, 0)),
        ] + s_out,
        out_shape=[
            jax.ShapeDtypeStruct((M, D), F32),
            jax.ShapeDtypeStruct((M, D), BF16),
        ] + s_shapes,
        scratch_shapes=[pltpu.VMEM((tm, D), BF16)],
        compiler_params=_params(1),
        name="mix",
    )(hm, proj, proj, proj, proj, x2, conv_w, w_bm, w_bc, w_out, g2, *s_arrays)
    return outs[0], outs[1], outs[2:]


def _ffn_kernel(hn_ref, h1_ref, wu_ref, cw_ref, cb_ref, wd_ref, g3_ref, out_ref,
                u_ref, a_ref, carry_ref, *, nj, blocks_per_seq, norm_chunk=64):
    s = pl.program_id(0)
    tm, tn = hn_ref.shape[0], wd_ref.shape[0]
    tile_b = jnp.maximum(s - 1, 0)
    ib, jb = tile_b // nj, tile_b % nj
    jc = (s + nj - 2) % nj

    @pl.when(s == 0)
    def _():
        u_ref[...] = jnp.zeros_like(u_ref)
        a_ref[...] = jnp.zeros_like(a_ref)
        carry_ref[...] = jnp.zeros_like(carry_ref)

    @pl.when((jc == 0) | (s == 0))
    def _():
        out_ref[...] = h1_ref[...]

    out_ref[...] += jnp.dot(a_ref[...], wd_ref[...], preferred_element_type=F32)

    prev = jnp.where(ib % blocks_per_seq == 0, 0.0, carry_ref[jb])
    carry_ref[jb] = u_ref[tm - HALO:tm, :]
    y = _causal_conv3(u_ref[...], prev, cw_ref) + cb_ref[...]
    gate, val = y[:, :tn], y[:, tn:]
    a_ref[...] = (gate * _sigmoid(gate) * val).astype(BF16)

    u_ref[...] = jnp.dot(hn_ref[...], wu_ref[...], preferred_element_type=F32)

    @pl.when((jc == nj - 1) & (s >= 2))
    def _():
        def body(r, carry):
            rows = pl.ds(pl.multiple_of(r * norm_chunk, norm_chunk), norm_chunk)
            h2 = out_ref[rows, :]
            rs = lax.rsqrt(jnp.mean(h2 * h2, axis=-1, keepdims=True) + EPS)
            out_ref[rows, :] = h2 * rs * g3_ref[...]
            return carry
        lax.fori_loop(0, tm // norm_chunk, body, 0, unroll=True)


def _ffn(hn2, h1, w_up_t, cw_t, cb_t, w_down_p, g3, *, seq, tm=512):
    M, D = h1.shape
    nj, _, tn2 = w_up_t.shape
    tn = tn2 // 2
    assert M % tm == 0 and seq % tm == 0 and w_down_p.shape[0] == nj * tn
    ni = M // tm
    n_tiles = ni * nj
    row_a = lambda s: (jnp.minimum(s // nj, ni - 1), 0)
    row_c = lambda s: (jnp.maximum(s - 2, 0) // nj, 0)
    tile_a = lambda s: (s % nj, 0, 0)
    tile_b = lambda s: (jnp.maximum(s - 1, 0) % nj, 0, 0)
    return pl.pallas_call(
        functools.partial(_ffn_kernel, nj=nj, blocks_per_seq=seq // tm),
        grid=(n_tiles + 2,),
        in_specs=[
            pl.BlockSpec((tm, D), row_a),
            pl.BlockSpec((tm, D), row_c),
            pl.BlockSpec((None, D, tn2), tile_a),
            pl.BlockSpec((None, 3, tn2), tile_b),
            pl.BlockSpec((None, 1, tn2), tile_b),
            pl.BlockSpec((tn, D), lambda s: ((s + nj - 2) % nj, 0)),
            pl.BlockSpec((1, D), lambda s: (0, 0)),
        ],
        out_specs=pl.BlockSpec((tm, D), row_c),
        out_shape=jax.ShapeDtypeStruct((M, D), F32),
        scratch_shapes=[
            pltpu.VMEM((tm, tn2), F32),
            pltpu.VMEM((tm, tn), BF16),
            pltpu.VMEM((nj, HALO, tn2), F32),
        ],
        compiler_params=_params(1),
        name="ffn",
    )(hn2, h1, w_up_t, cw_t, cb_t, w_down_p, g3)


def _win_prep_kernel(w_ref, tail_ref, gate_ref, o_ref, og_ref, *, first_shifted):
    i = pl.program_id(0)
    m = w_ref[...]
    shifted = jnp.concatenate([m[GATE_COLS:, :], tail_ref[...]], axis=0)
    o_ref[...] = jnp.where(i >= first_shifted, shifted, m).astype(BF16)

    gr = gate_ref[...]
    lead = lax.broadcasted_iota(jnp.int32, gr.shape, 0) < M_HEADS
    zeros = jnp.zeros((LANES - GATE_COLS, gr.shape[1]), F32)
    og_ref[...] = jnp.concatenate(
        [jnp.where(lead, gr, 0.0), zeros,
         jnp.where(lead, pltpu.roll(gr, M_HEADS, 0), 0.0), zeros], axis=0).astype(BF16)


def _win_prep(w_in_t, *, rows=512):
    n, D = w_in_t.shape
    n_main = n - GATE_COLS
    assert n_main % rows == 0 and MAIN_SPLIT % rows == 0 and rows % GATE_COLS == 0
    return pl.pallas_call(
        functools.partial(_win_prep_kernel, first_shifted=MAIN_SPLIT // rows),
        grid=(n_main // rows,),
        in_specs=[
            pl.BlockSpec((rows, D), lambda i: (i, 0)),
            pl.BlockSpec((GATE_COLS, D), lambda i: ((i + 1) * (rows // GATE_COLS), 0)),
            pl.BlockSpec((GATE_COLS, D), lambda i: (MAIN_SPLIT // GATE_COLS, 0)),
        ],
        out_specs=[
            pl.BlockSpec((rows, D), lambda i: (i, 0)),
            pl.BlockSpec((2 * LANES, D), lambda i: (0, 0)),
        ],
        out_shape=[
            jax.ShapeDtypeStruct((n_main, D), BF16),
            jax.ShapeDtypeStruct((2 * LANES, D), BF16),
        ],
        compiler_params=_params(1),
        name="win_prep",
    )(w_in_t, w_in_t, w_in_t)


def _row_chunks(total, n_chunks):
    rows = total // n_chunks
    assert rows * n_chunks == total and rows % 16 == 0
    return rows


def _cast_stream(w, n_chunks):
    r = _row_chunks(w.shape[0], n_chunks)
    return (w, _cast_rows, (r, w.shape[1]), (r, w.shape[1]), w.shape, n_chunks)


def _w_down_stream(w_down, ffp, n_chunks):
    d_ff, D = w_down.shape
    r = _row_chunks(ffp, n_chunks)
    assert (n_chunks - 1) * r < d_ff
    return (w_down, _cast_rows_zero_tail(d_ff), (r, D), (r, D), (ffp, D), n_chunks)


def _w_up_stream(w_up, d_ff, tn, n_chunks):
    D = w_up.shape[0]
    nj = pl.cdiv(d_ff, tn)
    assert d_ff % LANES == 0 and tn % LANES == 0
    r = _row_chunks(D, n_chunks)
    return (w_up, _cast_tile_pairs(d_ff, tn), (r, 2 * d_ff), (nj, r, 2 * tn), (nj, D, 2 * tn),
            n_chunks)


def _tile_pairs(a, d_ff, tn):
    nj = pl.cdiv(d_ff, tn)
    pad = lambda h: jnp.pad(h, ((0, 0), (0, nj * tn - d_ff))).reshape(a.shape[0], nj, tn)
    both = jnp.concatenate([pad(a[:, :d_ff]), pad(a[:, d_ff:])], axis=2)
    return jnp.transpose(both, (1, 0, 2))


def kernel(x, norm_mix_g, w_in, b_in, mlstm_head_g, w_branch_m, conv_w, w_branch_c, w_out,
           norm_ffn_g, w_up, ffn_conv_w, ffn_conv_b, w_down, norm_out_g):
    B, T, D = x.shape
    M = B * T
    d_ff = w_down.shape[0]
    ff_tile = 512
    x2 = x.reshape(M, D)

    g0, g1 = MAIN_SPLIT, MAIN_SPLIT + GATE_COLS
    gi = g0 + M_HEADS
    w_in_t = w_in.T
    w_main_t, w_gate_t = _win_prep(w_in_t)
    lead = lambda a: jnp.pad(a, ((0, LANES - M_HEADS), (0, 0)))
    b_main = jnp.concatenate([b_in[:g0], b_in[g1:]])[None, :]
    b_gate = jnp.concatenate([lead(b_in[g0:gi, None]), lead(b_in[gi:g1, None])], axis=0).T
    cw_t = _tile_pairs(ffn_conv_w, d_ff, ff_tile)
    cb_t = _tile_pairs(ffn_conv_b[None, :], d_ff, ff_tile)

    ffp = pl.cdiv(d_ff, ff_tile) * ff_tile
    proj, gates, (w_down_p,) = _inproj(x2, norm_mix_g[None, :], w_main_t, b_main, w_gate_t, b_gate,
                                       [_w_down_stream(w_down, ffp, 32)])
    chunk = 256
    hm, (w_out_b, w_bm_b, w_bc_b) = _mlstm(
        proj, gates, mlstm_head_g[None, :],
        [_cast_stream(w, T // chunk) for w in (w_out, w_branch_m, w_branch_c)],
        batch=B, seq=T, chunk=chunk)
    h1, hn2, (w_up_t,) = _mix(hm, proj, x2, conv_w, w_bm_b, w_bc_b, w_out_b, norm_ffn_g[None, :],
                              [_w_up_stream(w_up, d_ff, ff_tile, 32)], seq=T)
    out = _ffn(hn2, h1, w_up_t, cw_t, cb_t, w_down_p, norm_out_g[None, :], seq=T)
    return out.reshape(B, T, D)
```
